```python
import math
import jax, jax.numpy as jnp
from jax import lax
import numpy as np


D_MODEL = 2048
BATCH = 4
SEQ = 4096
DEPTH = 4

RWKV_HEADS = 16
RWKV_HEAD_DIM = 64
RWKV_DIM = RWKV_HEADS * RWKV_HEAD_DIM
DECAY_LORA = 64
AAA_LORA = 64
GATE_LORA = 160
RWKV_GN_EPS = 64e-5

SG_GROUPS = 4
SG_GROUP_DIM = 128
SG_DIM = SG_GROUPS * SG_GROUP_DIM
SG_CHUNK = 128

ATT_HEADS = 8
ATT_HEAD_DIM = 64
ATT_DIM = ATT_HEADS * ATT_HEAD_DIM
IDX_HEADS = 16
IDX_HEAD_DIM = 64
TOPK_MAX = 256
Q_BLOCK = 128

REL_BUCKETS = 32
REL_MAX_DIST = 128

N_BRANCH = 3
D_MIX = RWKV_DIM + SG_DIM + ATT_DIM
D_FF = 5632
LN_EPS = 1e-5
DEEPNORM_ALPHA = (2 * DEPTH) ** 0.25
DEEPNORM_BETA = (8 * DEPTH) ** -0.25

RWKV_SPLIT_POINTS = (RWKV_DIM, 2 * RWKV_DIM, 3 * RWKV_DIM,
                     3 * RWKV_DIM + DECAY_LORA, 3 * RWKV_DIM + DECAY_LORA + AAA_LORA)
RWKV_COLS = 3 * RWKV_DIM + DECAY_LORA + AAA_LORA + GATE_LORA
SG_COLS = 2 * SG_DIM
ATT_SPLIT_POINTS = (ATT_DIM, 2 * ATT_DIM, 3 * ATT_DIM,
                    3 * ATT_DIM + IDX_HEADS * IDX_HEAD_DIM,
                    3 * ATT_DIM + IDX_HEADS * IDX_HEAD_DIM + IDX_HEAD_DIM)
ATT_COLS = 3 * ATT_DIM + IDX_HEADS * IDX_HEAD_DIM + IDX_HEAD_DIM + IDX_HEADS
GATE_COLS = N_BRANCH * D_MODEL
GROUP_SPLIT_POINTS = (RWKV_COLS, RWKV_COLS + SG_COLS, RWKV_COLS + SG_COLS + ATT_COLS)
D_IN = RWKV_COLS + SG_COLS + ATT_COLS + GATE_COLS

kernel_name = "hybrid_rwkv7_sgu_dsa_macaron_deepnorm"


def layer_norm(x, g, b, eps=LN_EPS):
    xf = x.astype(jnp.float32)
    mu = jnp.mean(xf, axis=-1, keepdims=True)
    var = jnp.mean(jnp.square(xf - mu), axis=-1, keepdims=True)
    return ((xf - mu) * lax.rsqrt(var + eps) * g + b).astype(x.dtype)


def swiglu_ffn(x, w_in, w_out):
    gate, up = jnp.split(x @ w_in, 2, axis=-1)
    return (jax.nn.silu(gate) * up) @ w_out


def rel_bucket(dist):
    max_exact = REL_BUCKETS // 2
    d_f = jnp.maximum(dist, 1).astype(jnp.float32)
    large = max_exact + (jnp.log(d_f / max_exact) / math.log(REL_MAX_DIST / max_exact)
                         * (REL_BUCKETS - max_exact)).astype(jnp.int32)
    large = jnp.minimum(large, REL_BUCKETS - 1)
    return jnp.where(dist < max_exact, dist, large)


def rwkv7_time_mix(p, mu, w0, w2, a0, a2, g2, k_k, k_a, r_k, gn_g, gn_b):
    B, S, _ = p.shape
    H, N = RWKV_HEADS, RWKV_HEAD_DIM
    p_prev = jnp.pad(p, ((0, 0), (1, 0), (0, 0)))[:, :-1]
    p = p + (p_prev - p) * mu
    r, k, v, wd, ad, gd = jnp.split(p, RWKV_SPLIT_POINTS, axis=-1)
    wlog = -jax.nn.softplus(-(w0 + jnp.tanh(wd) @ w2)) - 0.5
    decay = jnp.exp(-jnp.exp(wlog.astype(jnp.float32)))
    a = jax.nn.sigmoid(a0 + ad @ a2)
    g = jax.nn.sigmoid(gd) @ g2

    def heads(t):
        return t.astype(jnp.float32).reshape(B, S, H, N)

    kk = heads(k * k_k)
    kk = kk / jnp.maximum(jnp.sqrt(jnp.sum(jnp.square(kk), axis=-1, keepdims=True)), 1e-12)
    k = k * (1 + (a - 1) * k_a)
    r_h, k_h, v_h, a_h, w_h = heads(r), heads(k), heads(v), heads(a), decay.reshape(B, S, H, N)
    xs = tuple(jnp.moveaxis(t, 1, 0) for t in (r_h, w_h, k_h, v_h, -kk, kk * a_h))

    def step(state, inp):
        r_t, w_t, k_t, v_t, aa_t, bb_t = inp
        sa = jnp.einsum('bhij,bhj->bhi', state, aa_t)
        state = (state * w_t[:, :, None, :] + sa[..., None] * bb_t[:, :, None, :]
                 + v_t[..., None] * k_t[:, :, None, :])
        return state, jnp.einsum('bhij,bhj->bhi', state, r_t)

    _, y = lax.scan(step, jnp.zeros((B, H, N, N), jnp.float32), xs)
    y = jnp.moveaxis(y, 0, 1)
    y_mu = jnp.mean(y, axis=-1, keepdims=True)
    y_var = jnp.mean(jnp.square(y - y_mu), axis=-1, keepdims=True)
    y = ((y - y_mu) * lax.rsqrt(y_var + RWKV_GN_EPS)).reshape(B, S, RWKV_DIM) * gn_g + gn_b
    bonus = jnp.sum(r_h * k_h * r_k, axis=-1, keepdims=True) * v_h
    y = y + bonus.reshape(B, S, RWKV_DIM)
    return (y * g).astype(p.dtype)


def spatial_gating(p, ln_g, ln_b, w_s, b_s):
    B, S, _ = p.shape
    z = jax.nn.gelu(p, approximate=False)
    u, v = z[..., :SG_DIM], z[..., SG_DIM:]
    v = layer_norm(v, ln_g, ln_b)
    v = v.reshape(B, S // SG_CHUNK, SG_CHUNK, SG_GROUPS, SG_GROUP_DIM)
    causal = jnp.tril(jnp.ones((SG_CHUNK, SG_CHUNK), dtype=bool))
    w = jnp.where(causal[None], w_s, jnp.zeros_like(w_s))
    mixed = jnp.einsum('gij,bcjgd->bcigd', w, v) + jnp.transpose(b_s)[:, :, None]
    return u * mixed.reshape(B, S, SG_DIM)


def dsa_attention(p, idx_ln_g, idx_ln_b, rel_bias):
    B, S, _ = p.shape
    top_k = min(TOPK_MAX, S // 4)
    nb = S // Q_BLOCK
    q, k, v, q_idx, k_idx, w_idx = jnp.split(p, ATT_SPLIT_POINTS, axis=-1)
    q = q.reshape(B, S, ATT_HEADS, ATT_HEAD_DIM)
    k = k.reshape(B, S, ATT_HEADS, ATT_HEAD_DIM)
    v = v.reshape(B, S, ATT_HEADS, ATT_HEAD_DIM)
    q_idx = q_idx.reshape(B, S, IDX_HEADS, IDX_HEAD_DIM)
    k_idx = layer_norm(k_idx, idx_ln_g, idx_ln_b)
    w_idx = w_idx * IDX_HEADS ** -0.5
    key_pos = jnp.arange(S, dtype=jnp.int32)

    def to_blocks(t):
        return jnp.moveaxis(t.reshape(B, nb, Q_BLOCK, *t.shape[2:]), 1, 0)

    def block(args):
        qb, qib, wb, start = args
        q_pos = start + jnp.arange(Q_BLOCK, dtype=jnp.int32)
        causal = key_pos[None, :] <= q_pos[:, None]
        dots = jnp.einsum('bqhd,bsd->bqhs', qib, k_idx) * IDX_HEAD_DIM ** -0.5
        score = jnp.einsum('bqh,bqhs->bqs', wb, jax.nn.relu(dots)).astype(jnp.float32)
        score = jnp.where(causal[None], score, -jnp.inf)
        _, sel = lax.top_k(score, top_k)
        k_sel = jax.vmap(lambda kb, ib: kb[ib])(k, sel)
        v_sel = jax.vmap(lambda vb, ib: vb[ib])(v, sel)
        dist = q_pos[None, :, None] - sel
        bias = rel_bias[rel_bucket(jnp.maximum(dist, 0))]
        logits = (jnp.einsum('bqhd,bqkhd->bqhk', qb, k_sel).astype(jnp.float32)
                  * ATT_HEAD_DIM ** -0.5 + jnp.swapaxes(bias, -1, -2))
        logits = jnp.where((dist >= 0)[:, :, None, :], logits, -jnp.inf)
        probs = jax.nn.softmax(logits, axis=-1).astype(v.dtype)
        return jnp.einsum('bqhk,bqkhd->bqhd', probs, v_sel)

    starts = jnp.arange(nb, dtype=jnp.int32) * Q_BLOCK
    out = lax.map(block, (to_blocks(q), to_blocks(q_idx), to_blocks(w_idx), starts))
    return jnp.moveaxis(out, 0, 1).reshape(B, S, ATT_DIM)


def hybrid_token_mixer(x, w_in, b_gate, rwkv_mu, rwkv_w0, rwkv_w2, rwkv_a0, rwkv_a2, rwkv_g2,
                       rwkv_k_k, rwkv_k_a, rwkv_r_k, rwkv_gn_g, rwkv_gn_b,
                       sg_ln_g, sg_ln_b, sg_w, sg_b, idx_ln_g, idx_ln_b, rel_bias,
                       w_branch, w_o):
    B, S, _ = x.shape
    p_rwkv, p_sg, p_att, p_gate = jnp.split(x @ w_in, GROUP_SPLIT_POINTS, axis=-1)
    y_rwkv = rwkv7_time_mix(p_rwkv, rwkv_mu, rwkv_w0, rwkv_w2, rwkv_a0, rwkv_a2, rwkv_g2,
                            rwkv_k_k, rwkv_k_a, rwkv_r_k, rwkv_gn_g, rwkv_gn_b)
    y_sg = spatial_gating(p_sg, sg_ln_g, sg_ln_b, sg_w, sg_b)
    y_att = dsa_attention(p_att, idx_ln_g, idx_ln_b, rel_bias)
    gates = jax.nn.sigmoid(p_gate + b_gate).reshape(B, S, N_BRANCH, D_MODEL)
    z_rwkv = y_rwkv @ w_branch[:RWKV_DIM]
    z_sg = y_sg @ w_branch[RWKV_DIM:RWKV_DIM + SG_DIM]
    z_att = y_att @ w_branch[RWKV_DIM + SG_DIM:]
    merged = gates[:, :, 0] * z_rwkv + gates[:, :, 1] * z_sg + gates[:, :, 2] * z_att
    return merged @ w_o


def setup_inputs(seed: int = 0) -> dict:
    key = jax.random.key(seed)
    keys = jax.random.split(key, 40)
    counter = [0]

    def nxt():
        counter[0] += 1
        return keys[counter[0] - 1]

    def normal(shape, scale):
        return scale * jax.random.normal(nxt(), shape, jnp.float32)

    def uniform(shape, lo, hi):
        return jax.random.uniform(nxt(), shape, jnp.float32, lo, hi)

    L = DEPTH
    return {
        'x': normal((BATCH, SEQ, D_MODEL), 1.0),
        'ffn1_w_in': normal((L, D_MODEL, 2 * D_FF), D_MODEL ** -0.5),
        'ffn1_w_out': normal((L, D_FF, D_MODEL), DEEPNORM_BETA * D_FF ** -0.5),
        'ln1_g': 1.0 + normal((L, D_MODEL), 0.02),
        'ln1_b': normal((L, D_MODEL), 0.02),
        'w_in': normal((L, D_MODEL, D_IN), D_MODEL ** -0.5),
        'b_gate': normal((L, GATE_COLS), 0.1),
        'rwkv_mu': uniform((L, RWKV_COLS), 0.2, 0.8),
        'rwkv_w0': uniform((L, RWKV_DIM), -6.0, 0.0),
        'rwkv_w2': normal((L, DECAY_LORA, RWKV_DIM), 0.1),
        'rwkv_a0': normal((L, RWKV_DIM), 0.1),
        'rwkv_a2': normal((L, AAA_LORA, RWKV_DIM), AAA_LORA ** -0.5),
        'rwkv_g2': normal((L, GATE_LORA, RWKV_DIM), GATE_LORA ** -0.5),
        'rwkv_k_k': 0.85 + normal((L, RWKV_DIM), 0.02),
        'rwkv_k_a': 1.0 + normal((L, RWKV_DIM), 0.02),
        'rwkv_r_k': normal((L, RWKV_HEADS, RWKV_HEAD_DIM), 0.1),
        'rwkv_gn_g': 1.0 + normal((L, RWKV_DIM), 0.02),
        'rwkv_gn_b': normal((L, RWKV_DIM), 0.02),
        'sg_ln_g': 1.0 + normal((L, SG_DIM), 0.02),
        'sg_ln_b': normal((L, SG_DIM), 0.02),
        'sg_w': normal((L, SG_GROUPS, SG_CHUNK, SG_CHUNK), SG_CHUNK ** -0.5),
        'sg_b': 1.0 + normal((L, SG_GROUPS, SG_CHUNK), 0.1),
        'idx_ln_g': 1.0 + normal((L, IDX_HEAD_DIM), 0.02),
        'idx_ln_b': normal((L, IDX_HEAD_DIM), 0.02),
        'rel_bias': normal((REL_BUCKETS, ATT_HEADS), 0.5),
        'w_branch': normal((L, D_MIX, D_MODEL), D_MIX ** -0.5),
        'w_o': normal((L, D_MODEL, D_MODEL), DEEPNORM_BETA * D_MODEL ** -0.5),
        'ln2_g': 1.0 + normal((L, D_MODEL), 0.02),
        'ln2_b': normal((L, D_MODEL), 0.02),
        'ffn2_w_in': normal((L, D_MODEL, 2 * D_FF), D_MODEL ** -0.5),
        'ffn2_w_out': normal((L, D_FF, D_MODEL), DEEPNORM_BETA * D_FF ** -0.5),
        'ln3_g': 1.0 + normal((L, D_MODEL), 0.02),
        'ln3_b': normal((L, D_MODEL), 0.02),
    }


def reference(x, ffn1_w_in, ffn1_w_out, ln1_g, ln1_b, w_in, b_gate, rwkv_mu, rwkv_w0, rwkv_w2,
              rwkv_a0, rwkv_a2, rwkv_g2, rwkv_k_k, rwkv_k_a, rwkv_r_k, rwkv_gn_g, rwkv_gn_b,
              sg_ln_g, sg_ln_b, sg_w, sg_b, idx_ln_g, idx_ln_b, rel_bias, w_branch, w_o,
              ln2_g, ln2_b, ffn2_w_in, ffn2_w_out, ln3_g, ln3_b):
    for l in range(DEPTH):
        x = layer_norm(DEEPNORM_ALPHA * x + 0.5 * swiglu_ffn(x, ffn1_w_in[l], ffn1_w_out[l]),
                       ln1_g[l], ln1_b[l])
        mix = hybrid_token_mixer(x, w_in[l], b_gate[l], rwkv_mu[l], rwkv_w0[l], rwkv_w2[l],
                                 rwkv_a0[l], rwkv_a2[l], rwkv_g2[l], rwkv_k_k[l], rwkv_k_a[l],
                                 rwkv_r_k[l], rwkv_gn_g[l], rwkv_gn_b[l], sg_ln_g[l], sg_ln_b[l],
                                 sg_w[l], sg_b[l], idx_ln_g[l], idx_ln_b[l], rel_bias,
                                 w_branch[l], w_o[l])
        x = layer_norm(DEEPNORM_ALPHA * x + mix, ln2_g[l], ln2_b[l])
        x = layer_norm(DEEPNORM_ALPHA * x + 0.5 * swiglu_ffn(x, ffn2_w_in[l], ffn2_w_out[l]),
                       ln3_g[l], ln3_b[l])
    return x
```

```python
import functools
import math

import jax
import jax.numpy as jnp
from jax import lax
from jax.experimental import pallas as pl
from jax.experimental.pallas import tpu as pltpu

F32 = jnp.float32
BF16 = jnp.bfloat16
I32 = jnp.int32

V7X_VMEM_BYTES = 64 * 1024 * 1024
VMEM_LIMIT_BYTES = V7X_VMEM_BYTES - 8 * 1024 * 1024
LANES = 128

D_MODEL = 2048
DEPTH = 4
RWKV_HEADS, RWKV_HEAD_DIM = 16, 64
RWKV_DIM = RWKV_HEADS * RWKV_HEAD_DIM
DECAY_LORA, AAA_LORA, GATE_LORA = 64, 64, 160
RWKV_GN_EPS = 64e-5
SG_GROUPS, SG_GROUP_DIM, SG_CHUNK = 4, 128, 128
SG_DIM = SG_GROUPS * SG_GROUP_DIM
ATT_HEADS, ATT_HEAD_DIM = 8, 64
ATT_DIM = ATT_HEADS * ATT_HEAD_DIM
IDX_HEADS, IDX_HEAD_DIM = 16, 64
TOPK_MAX = 256
Q_BLOCK = 128
REL_BUCKETS, REL_MAX_DIST = 32, 128
D_FF = 5632
LN_EPS = 1e-5
DEEPNORM_ALPHA = (2 * DEPTH) ** 0.25

RWKV_COLS = 3 * RWKV_DIM + DECAY_LORA + AAA_LORA + GATE_LORA
RWKV_LORA_PAD = 384
RWKV_COLS_PAD = 3 * RWKV_DIM + RWKV_LORA_PAD
SG_COLS = 2 * SG_DIM
ATT_MAIN_COLS = IDX_HEADS * IDX_HEAD_DIM + 3 * ATT_DIM
ATT_COLS = 3 * ATT_DIM + IDX_HEADS * IDX_HEAD_DIM + IDX_HEAD_DIM + IDX_HEADS
GATE_COLS = 3 * D_MODEL

RWKV_CHUNK = 64
NEG_BIG = -1e30


def _cparams(*sem):
    return pltpu.CompilerParams(dimension_semantics=sem, vmem_limit_bytes=VMEM_LIMIT_BYTES)


def _layer_norm(y, g, b, eps):
    mu = jnp.mean(y, axis=-1, keepdims=True)
    d = y - mu
    var = jnp.mean(d * d, axis=-1, keepdims=True)
    return d * lax.rsqrt(var + eps) * g + b


def _dot(a, b):
    return jnp.dot(a, b, preferred_element_type=F32)


def _dot_nt(a, b):
    return lax.dot_general(a, b, (((1,), (1,)), ((), ())), preferred_element_type=F32)


def _ffn_kernel(x_ref, wg_ref, wu_ref, wo_ref, g_ref, b_ref, o_ref, ob_ref, xb_ref, acc_ref):
    j = pl.program_id(1)

    @pl.when(j == 0)
    def _():
        xb_ref[...] = x_ref[...].astype(BF16)
        acc_ref[...] = jnp.zeros_like(acc_ref)

    xb = xb_ref[...]
    gate = _dot(xb, wg_ref[...])
    up = _dot(xb, wu_ref[...])
    h = (gate * jax.nn.sigmoid(gate) * up).astype(BF16)
    acc_ref[...] += _dot(h, wo_ref[...])

    @pl.when(j == pl.num_programs(1) - 1)
    def _():
        y = DEEPNORM_ALPHA * x_ref[...] + 0.5 * acc_ref[...]
        out = _layer_norm(y, g_ref[...], b_ref[...], LN_EPS)
        o_ref[...] = out
        ob_ref[...] = out.astype(BF16)


def _ffn_ln(x, w_in, w_out, g, b, *, tm=512, tf=512):
    t, d = x.shape
    f = w_out.shape[0]
    nj = f // tf
    return pl.pallas_call(
        _ffn_kernel,
        grid=(t // tm, nj),
        in_specs=[
            pl.BlockSpec((tm, d), lambda i, j: (i, 0)),
            pl.BlockSpec((d, tf), lambda i, j: (0, j)),
            pl.BlockSpec((d, tf), lambda i, j: (0, j + nj)),
            pl.BlockSpec((tf, d), lambda i, j: (j, 0)),
            pl.BlockSpec((1, d), lambda i, j: (0, 0)),
            pl.BlockSpec((1, d), lambda i, j: (0, 0)),
        ],
        out_specs=[
            pl.BlockSpec((tm, d), lambda i, j: (i, 0)),
            pl.BlockSpec((tm, d), lambda i, j: (i, 0)),
        ],
        out_shape=[jax.ShapeDtypeStruct((t, d), F32), jax.ShapeDtypeStruct((t, d), BF16)],
        scratch_shapes=[pltpu.VMEM((tm, d), BF16), pltpu.VMEM((tm, d), F32)],
        compiler_params=_cparams("parallel", "arbitrary"),
        name="ffn_ln",
    )(x, w_in, w_in, w_out, g, b)


def _mm_kernel(x_ref, w_ref, o_ref):
    o_ref[...] = _dot(x_ref[...], w_ref[...]).astype(o_ref.dtype)


def _matmul(x, w, out_dtype, *, bm, bn, name):
    t, k = x.shape
    n = w.shape[1]
    return pl.pallas_call(
        _mm_kernel,
        grid=(t // bm, n // bn),
        in_specs=[pl.BlockSpec((bm, k), lambda i, j: (i, 0)), pl.BlockSpec((k, bn), lambda i, j: (0, j))],
        out_specs=pl.BlockSpec((bm, bn), lambda i, j: (i, j)),
        out_shape=jax.ShapeDtypeStruct((t, n), out_dtype),
        compiler_params=_cparams("parallel", "arbitrary"),
        name=name,
    )(x, w)


def _split3(x):
    h = x.astype(BF16)
    r1 = x - h.astype(F32)
    m = r1.astype(BF16)
    lo = (r1 - m.astype(F32)).astype(BF16)
    return h, m, lo


def _dot3(x, w_bf16):
    h, m, lo = _split3(x)
    return _dot(h, w_bf16) + _dot(m, w_bf16) + _dot(lo, w_bf16)


def _dot3_left(w_bf16, x):
    h, m, lo = _split3(x)
    return _dot(w_bf16, h) + _dot(w_bf16, m) + _dot(w_bf16, lo)


def _rwkv_kernel(r_ref, k_ref, v_ref, lo_ref, mur_ref, muk_ref, muv_ref, mulo_ref,
                 w0_ref, a0_ref, kk_ref, ka_ref, rk_ref, gg_ref, gb_ref,
                 w2_ref, a2_ref, g2_ref, o_ref,
                 s_ref, pr_ref, pk_ref, pv_ref, plo_ref,
                 rs_ref, lw_ref, k2_ref, vs_ref, kn_ref, as_ref, y_ref):
    tb = pl.program_id(2)
    ts = r_ref.shape[0]
    C = RWKV_CHUNK
    H2 = 2 * C

    @pl.when(tb == 0)
    def _():
        s_ref[...] = jnp.zeros_like(s_ref)
        pr_ref[...] = jnp.zeros_like(pr_ref)
        pk_ref[...] = jnp.zeros_like(pk_ref)
        pv_ref[...] = jnp.zeros_like(pv_ref)
        plo_ref[...] = jnp.zeros_like(plo_ref)

    def shift(x_ref, prev_ref, mu_ref):
        x = x_ref[...]
        first = lax.broadcasted_iota(I32, x.shape, 0) == 0
        xp = jnp.where(first, prev_ref[...], pltpu.roll(x, 1, 0))
        prev_ref[...] = x[ts - 1:ts, :]
        return x + (xp - x) * mu_ref[...]

    r = shift(r_ref, pr_ref, mur_ref)
    k = shift(k_ref, pk_ref, muk_ref)
    v = shift(v_ref, pv_ref, muv_ref)
    lo = shift(lo_ref, plo_ref, mulo_ref)

    lane = lax.broadcasted_iota(I32, (LANES, LANES), 1)
    row = lax.broadcasted_iota(I32, (LANES, LANES), 0)
    same_head = (lane < 64) == (row < 64)
    seg_ones = jnp.where(same_head, 1.0, 0.0).astype(BF16)

    wa = lo[:, :LANES]
    zw = w0_ref[...] + _dot(jnp.tanh(wa).astype(BF16), w2_ref[...])
    nz = -zw
    softplus = jnp.maximum(nz, 0.0) + jnp.log1p(jnp.exp(-jnp.abs(nz)))
    logw = -jnp.exp(-softplus - 0.5)
    a = jax.nn.sigmoid(a0_ref[...] + _dot(wa.astype(BF16), a2_ref[...]))
    g = _dot(jax.nn.sigmoid(lo[:, LANES:]).astype(BF16), g2_ref[...])

    kk = k * kk_ref[...]
    n2 = _dot3(kk * kk, seg_ones)
    kkn = kk / jnp.maximum(jnp.sqrt(n2), 1e-12)
    k2 = k * (1.0 + (a - 1.0) * ka_ref[...])

    rs_ref[...] = r
    lw_ref[...] = logw
    k2_ref[...] = k2
    vs_ref[...] = v
    kn_ref[...] = kkn
    as_ref[...] = a

    crow = lax.broadcasted_iota(I32, (C, H2), 0)
    ccol = lax.broadcasted_iota(I32, (C, H2), 1)
    ccol = jnp.where(ccol >= C, ccol - C, ccol)
    strict = ccol < crow
    incl = ccol <= crow
    eye_rp = jnp.where(ccol == crow, 1.0, 0.0)
    lt = jnp.where(lax.broadcasted_iota(I32, (C, C), 1) <= lax.broadcasted_iota(I32, (C, C), 0),
                   1.0, 0.0).astype(BF16)
    lane_c = lax.broadcasted_iota(I32, (C, LANES), 1)
    head_a = lane_c < 64
    eye128 = lane == row

    def stack(dm):
        return jnp.concatenate([jnp.where(head_a, dm, 0.0), jnp.where(head_a, 0.0, dm)], axis=0)

    def bdiag(x):
        return jnp.where(same_head, jnp.concatenate([x, x], axis=0), 0.0)

    def chunk(ci, carry):
        t0 = pl.multiple_of(ci * C, C)
        sl = pl.ds(t0, C)
        rc, lw, kc, vc, kn, ac = rs_ref[sl, :], lw_ref[sl, :], k2_ref[sl, :], vs_ref[sl, :], kn_ref[sl, :], as_ref[sl, :]
        aa = -kn
        bb = kn * ac
        cum = _dot3_left(lt, lw)
        cl = cum[C - 1:C, :]
        p_in = jnp.exp(cum)
        p_ex = jnp.exp(cum - lw)
        p_inv = jnp.exp(-cum)
        p_end = jnp.exp(cl - cum)
        at = aa * p_ex
        rt = rc * p_in
        kt = kc * p_inv
        bt = bb * p_inv
        kh = kc * p_end
        bh = bb * p_end

        ar = jnp.concatenate([at, rt], axis=0).astype(BF16)
        xb = _dot_nt(ar, stack(bt).astype(BF16))
        xk = _dot_nt(ar, stack(kt).astype(BF16))
        a_ab = jnp.where(strict, xb[:C], 0.0)
        a_rb = jnp.where(incl, xb[C:], 0.0)
        a_ak = jnp.where(strict, xk[:C], 0.0)
        a_rk = jnp.where(incl, xk[C:], 0.0)

        tinv = eye_rp + a_ab
        apow = a_ab
        n = 1
        while n < C // 2:
            apow = _dot(apow.astype(BF16), bdiag(apow).astype(BF16))
            tinv = tinv + _dot(tinv.astype(BF16), bdiag(apow).astype(BF16))
            n *= 2
        tinv_b = tinv.astype(BF16)

        gv = _dot(a_ak.astype(BF16), stack(vc).astype(BF16))
        w1 = _dot(tinv_b, stack(at).astype(BF16))
        w2 = _dot(tinv_b, stack(gv).astype(BF16))
        a_rb_b = a_rb.astype(BF16)
        rq = rt + _dot(a_rb_b, stack(w1).astype(BF16))
        y0 = _dot(a_rk.astype(BF16), stack(vc).astype(BF16)) + _dot(a_rb_b, stack(w2).astype(BF16))

        kbt = jnp.concatenate([kh, bh], axis=0).T.astype(BF16)
        n_add = jnp.where(same_head, _dot(kbt, jnp.concatenate([vc, w2], axis=0).astype(BF16)), 0.0)
        m_off = jnp.where(same_head, _dot(kbt, jnp.concatenate([jnp.zeros_like(w1), w1], axis=0).astype(BF16)), 0.0)
        m_mat = m_off + jnp.where(eye128, jnp.exp(cl), 0.0)

        s_b = s_ref[...].astype(BF16)
        y_ref[sl, :] = _dot(rq.astype(BF16), s_b) + y0
        s_ref[...] = _dot(m_mat.astype(BF16), s_b) + n_add
        return carry

    lax.fori_loop(0, ts // C, chunk, 0)

    y = y_ref[...]
    inv_n = 1.0 / RWKV_HEAD_DIM
    mu = _dot3(y, seg_ones) * inv_n
    d = y - mu
    var = _dot3(d * d, seg_ones) * inv_n
    yn = d * lax.rsqrt(var + RWKV_GN_EPS) * gg_ref[...] + gb_ref[...]
    bonus = _dot3(r * k2 * rk_ref[...], seg_ones) * v
    o_ref[...] = ((yn + bonus) * g).astype(o_ref.dtype)


def _rwkv(p, mu_pad, w0, a0, k_k, k_a, r_k, gn_g, gn_b, w2p, a2p, g2p, *, batch, seq, ts=512):
    t = p.shape[0]
    npair = RWKV_HEADS // 2
    nt = seq // ts
    nlora = 3 * RWKV_DIM // RWKV_LORA_PAD

    def col(off):
        return pl.BlockSpec((ts, LANES), lambda b, h, s, off=off: (b * nt + s, off + h))

    def prow(off=0):
        return pl.BlockSpec((1, LANES), lambda b, h, s, off=off: (0, off + h))

    f32 = lambda *shape: pltpu.VMEM(shape, F32)
    return pl.pallas_call(
        _rwkv_kernel,
        grid=(batch, npair, nt),
        in_specs=[
            col(0), col(npair), col(2 * npair),
            pl.BlockSpec((ts, RWKV_LORA_PAD), lambda b, h, s: (b * nt + s, nlora)),
            prow(0), prow(npair), prow(2 * npair),
            pl.BlockSpec((1, RWKV_LORA_PAD), lambda b, h, s: (0, nlora)),
            prow(), prow(), prow(), prow(), prow(), prow(), prow(),
            pl.BlockSpec((LANES, LANES), lambda b, h, s: (0, h)),
            pl.BlockSpec((LANES, LANES), lambda b, h, s: (0, h)),
            pl.BlockSpec((2 * LANES, LANES), lambda b, h, s: (0, h)),
        ],
        out_specs=pl.BlockSpec((ts, LANES), lambda b, h, s: (b * nt + s, h)),
        out_shape=jax.ShapeDtypeStruct((t, RWKV_DIM), BF16),
        scratch_shapes=[
            f32(LANES, LANES), f32(1, LANES), f32(1, LANES), f32(1, LANES), f32(1, RWKV_LORA_PAD),
            f32(ts, LANES), f32(ts, LANES), f32(ts, LANES), f32(ts, LANES), f32(ts, LANES), f32(ts, LANES),
            f32(ts, LANES),
        ],
        compiler_params=_cparams("parallel", "parallel", "arbitrary"),
        name="rwkv7",
    )(p, p, p, p, mu_pad, mu_pad, mu_pad, mu_pad, w0, a0, k_k, k_a, r_k, gn_g, gn_b, w2p, a2p, g2p)


def _sgu_kernel(p_ref, lg_ref, lb_ref, w_ref, bs_ref, o_ref):
    tm = p_ref.shape[0]
    p = p_ref[...]
    z = 0.5 * p * (1.0 + lax.erf(p * math.sqrt(0.5)))
    u = z[:, :SG_DIM]
    v = _layer_norm(z[:, SG_DIM:], lg_ref[...], lb_ref[...], LN_EPS).astype(BF16)
    rowi = lax.broadcasted_iota(I32, (SG_CHUNK, SG_CHUNK), 0)
    colj = lax.broadcasted_iota(I32, (SG_CHUNK, SG_CHUNK), 1)
    causal = colj <= rowi
    for g in range(SG_GROUPS):
        wg = jnp.where(causal, w_ref[g], 0.0).astype(BF16)
        gs = slice(g * SG_GROUP_DIM, (g + 1) * SG_GROUP_DIM)
        for c in range(tm // SG_CHUNK):
            rs = slice(c * SG_CHUNK, (c + 1) * SG_CHUNK)
            mixed = _dot(wg, v[rs, gs]) + bs_ref[:, gs]
            o_ref[rs, gs] = (u[rs, gs] * mixed).astype(o_ref.dtype)


def _sgu(p, ln_g, ln_b, w_s, bias_full, *, tm=512):
    t = p.shape[0]
    return pl.pallas_call(
        _sgu_kernel,
        grid=(t // tm,),
        in_specs=[
            pl.BlockSpec((tm, SG_COLS), lambda i: (i, 0)),
            pl.BlockSpec((1, SG_DIM), lambda i: (0, 0)),
            pl.BlockSpec((1, SG_DIM), lambda i: (0, 0)),
            pl.BlockSpec((SG_GROUPS, SG_CHUNK, SG_CHUNK), lambda i: (0, 0, 0)),
            pl.BlockSpec((SG_CHUNK, SG_DIM), lambda i: (0, 0)),
        ],
        out_specs=pl.BlockSpec((tm, SG_DIM), lambda i: (i, 0)),
        out_shape=jax.ShapeDtypeStruct((t, SG_DIM), BF16),
        compiler_params=_cparams("parallel"),
        name="sgu",
    )(p, ln_g, ln_b, w_s, bias_full)


def _kidx_kernel(x_ref, g_ref, b_ref, o_ref):
    x = x_ref[...]
    lane = lax.broadcasted_iota(I32, x.shape, 1)
    x2 = jnp.where(lane < IDX_HEAD_DIM, x, pltpu.roll(x, IDX_HEAD_DIM, 1))
    o_ref[...] = _layer_norm(x2, g_ref[...], b_ref[...], LN_EPS).astype(o_ref.dtype)


def _kidx_ln(x, g2, b2, *, tm=2048):
    t = x.shape[0]
    return pl.pallas_call(
        _kidx_kernel,
        grid=(t // tm,),
        in_specs=[pl.BlockSpec((tm, LANES), lambda i: (i, 0)),
                  pl.BlockSpec((1, LANES), lambda i: (0, 0)),
                  pl.BlockSpec((1, LANES), lambda i: (0, 0))],
        out_specs=pl.BlockSpec((tm, LANES), lambda i: (i, 0)),
        out_shape=jax.ShapeDtypeStruct((t, LANES), BF16),
        compiler_params=_cparams("parallel"),
        name="kidx_ln",
    )(x, g2, b2)


def _dsa_kernel(qi_ref, q_ref, k_ref, vt_ref, kx_ref, wt_ref, bn_ref, bf_ref, o_ref,
                rt_ref, qt_ref, key_ref, m_ref, l_ref, acc_ref, *, topk):
    i = pl.program_id(1)
    QB = Q_BLOCK
    row = lax.broadcasted_iota(I32, (QB, LANES), 0)
    lane = lax.broadcasted_iota(I32, (QB, LANES), 1)
    head_a = lane < 64

    def pair_rhs(x):
        zero = jnp.zeros_like(x)
        return jnp.concatenate([jnp.where(head_a, x, zero), jnp.where(head_a, zero, x)], axis=0)

    n_ip = IDX_HEADS // 2
    n_ap = ATT_HEADS // 2
    for hp in range(n_ip):
        rt_ref[hp] = pair_rhs(qi_ref[:, hp * LANES:(hp + 1) * LANES])
    for hp in range(n_ap):
        qt_ref[hp] = pair_rhs(q_ref[:, hp * LANES:(hp + 1) * LANES])

    scale = (IDX_HEAD_DIM ** -0.5) * (IDX_HEADS ** -0.5)
    wt = wt_ref[...] * scale
    wrows = [jnp.concatenate([wt[2 * hp:2 * hp + 1, :], wt[2 * hp + 1:2 * hp + 2, :]], axis=1)
             for hp in range(n_ip)]

    int_min = jnp.int32(-2 ** 31)

    def score_chunk(c, carry):
        ks = pl.ds(pl.multiple_of(c * QB, QB), QB)
        kc = kx_ref[ks, :]
        acc = jnp.zeros((QB, 2 * QB), F32)
        for hp in range(n_ip):
            acc = acc + jnp.maximum(_dot_nt(kc, rt_ref[hp]), 0.0) * wrows[hp]
        sc = acc[:, :QB] + acc[:, QB:]
        bits = lax.bitcast_convert_type(sc, I32)
        key = bits ^ ((bits >> 31) & jnp.int32(0x7FFFFFFF))
        valid = (c * QB + row) <= (i * QB + lane)
        key_ref[ks, :] = jnp.where(valid, key, int_min)
        return carry

    lax.fori_loop(0, i + 1, score_chunk, 0)

    def bit_step(bi, t):
        cand = t + lax.shift_left(jnp.int32(1), 31 - bi)

        def count_chunk(c, cnt):
            ks = pl.ds(pl.multiple_of(c * QB, QB), QB)
            ge = jnp.where(key_ref[ks, :] >= cand, 1, 0)
            part = ge[0:8, :]
            for r8 in range(1, QB // 8):
                part = part + ge[r8 * 8:(r8 + 1) * 8, :]
            return cnt + part

        cnt8 = lax.fori_loop(0, i + 1, count_chunk, jnp.zeros((8, LANES), I32))
        cnt = jnp.sum(cnt8, axis=0, keepdims=True)
        return jnp.where(cnt >= topk, cand, t)

    thr = lax.fori_loop(0, 32, bit_step, jnp.full((1, LANES), int_min, I32))

    m_ref[...] = jnp.full(m_ref.shape, NEG_BIG, F32)
    l_ref[...] = jnp.zeros(l_ref.shape, F32)
    acc_ref[...] = jnp.zeros(acc_ref.shape, F32)
    att_scale = ATT_HEAD_DIM ** -0.5

    def attend(c, bias_of, diag):
        ks = pl.ds(pl.multiple_of(c * QB, QB), QB)
        sel = key_ref[ks, :] >= thr
        if diag:
            sel = jnp.logical_and(sel, row <= lane)
        madd = jnp.where(sel, 0.0, 2.0 * NEG_BIG)
        madd2 = jnp.concatenate([madd, madd], axis=1)
        for hp in range(n_ap):
            kc2 = k_ref[ks, hp * LANES:(hp + 1) * LANES]
            s = _dot_nt(kc2, qt_ref[hp]) * att_scale + bias_of(hp) + madd2
            m_old = m_ref[hp]
            m_new = jnp.maximum(m_old, jnp.max(s, axis=0, keepdims=True))
            p = jnp.exp(s - m_new)
            alpha = jnp.exp(m_old - m_new)
            l_ref[hp] = alpha * l_ref[hp] + jnp.sum(p, axis=0, keepdims=True)
            vt = vt_ref[0, c, hp * LANES:(hp + 1) * LANES, :]
            acc_ref[hp] = acc_ref[hp] * alpha + _dot(vt, p.astype(BF16))
            m_ref[hp] = m_new

    def far_chunk(c, carry):
        attend(c, lambda hp: bf_ref[hp], False)
        return carry

    lax.fori_loop(0, jnp.maximum(i - 1, 0), far_chunk, 0)

    @pl.when(i >= 1)
    def _():
        attend(i - 1, lambda hp: bn_ref[1, hp], False)

    attend(i, lambda hp: bn_ref[0, hp], True)

    for hp in range(n_ap):
        o2 = acc_ref[hp] * (1.0 / l_ref[hp])
        pair_t = jnp.concatenate([o2[:64, :QB], o2[64:, QB:]], axis=0)
        o_ref[:, hp * LANES:(hp + 1) * LANES] = pair_t.T.astype(o_ref.dtype)


def _dsa(p_main, vt, kidx2, w_t, bias_near, bias_far, *, batch, seq):
    t = p_main.shape[0]
    nb = seq // Q_BLOCK
    top_k = min(TOPK_MAX, seq // 4)
    n_ip, n_ap = IDX_HEADS // 2, ATT_HEADS // 2
    qi_w = IDX_HEADS * IDX_HEAD_DIM
    return pl.pallas_call(
        functools.partial(_dsa_kernel, topk=top_k),
        grid=(batch, nb),
        in_specs=[
            pl.BlockSpec((Q_BLOCK, qi_w), lambda b, i: (b * nb + i, 0)),
            pl.BlockSpec((Q_BLOCK, ATT_DIM), lambda b, i: (b * nb + i, qi_w // ATT_DIM)),
            pl.BlockSpec((seq, ATT_DIM), lambda b, i: (b, qi_w // ATT_DIM + 1)),
            pl.BlockSpec((1, nb, ATT_DIM, Q_BLOCK), lambda b, i: (b, 0, 0, 0)),
            pl.BlockSpec((seq, LANES), lambda b, i: (b, 0)),
            pl.BlockSpec((IDX_HEADS, Q_BLOCK), lambda b, i: (0, b * nb + i)),
            pl.BlockSpec((2, n_ap, Q_BLOCK, 2 * Q_BLOCK), lambda b, i: (0, 0, 0, 0)),
            pl.BlockSpec((n_ap, 1, 2 * Q_BLOCK), lambda b, i: (0, 0, 0)),
        ],
        out_specs=pl.BlockSpec((Q_BLOCK, ATT_DIM), lambda b, i: (b * nb + i, 0)),
        out_shape=jax.ShapeDtypeStruct((t, ATT_DIM), BF16),
        scratch_shapes=[
            pltpu.VMEM((n_ip, 2 * Q_BLOCK, LANES), BF16),
            pltpu.VMEM((n_ap, 2 * Q_BLOCK, LANES), BF16),
            pltpu.VMEM((seq, Q_BLOCK), I32),
            pltpu.VMEM((n_ap, 1, 2 * Q_BLOCK), F32),
            pltpu.VMEM((n_ap, 1, 2 * Q_BLOCK), F32),
            pltpu.VMEM((n_ap, LANES, 2 * Q_BLOCK), F32),
        ],
        compiler_params=_cparams("parallel", "arbitrary"),
        name="dsa",
    )(p_main, p_main, p_main, vt, kidx2, w_t, bias_near, bias_far)


def _merge_kernel(yr_ref, ys_ref, ya_ref, pg_ref, bg_ref, wb_ref, wo_ref, x_ref, g_ref, b_ref, o_ref):
    d = D_MODEL
    z = _dot(yr_ref[...], wb_ref[0:RWKV_DIM, :])
    merged = jax.nn.sigmoid(pg_ref[:, 0:d] + bg_ref[:, 0:d]) * z
    z = _dot(ys_ref[...], wb_ref[RWKV_DIM:RWKV_DIM + SG_DIM, :])
    merged = merged + jax.nn.sigmoid(pg_ref[:, d:2 * d] + bg_ref[:, d:2 * d]) * z
    z = _dot(ya_ref[...], wb_ref[RWKV_DIM + SG_DIM:, :])
    merged = merged + jax.nn.sigmoid(pg_ref[:, 2 * d:] + bg_ref[:, 2 * d:]) * z
    mix = _dot(merged.astype(BF16), wo_ref[...])
    y = DEEPNORM_ALPHA * x_ref[...] + mix
    o_ref[...] = _layer_norm(y, g_ref[...], b_ref[...], LN_EPS)


def _merge(y_rwkv, y_sg, y_att, p_gate, b_gate, w_branch, w_o, x, g, b, *, tm=256):
    t, d = x.shape
    rowblk = lambda w: pl.BlockSpec((tm, w), lambda i: (i, 0))
    const = lambda r, c: pl.BlockSpec((r, c), lambda i: (0, 0))
    return pl.pallas_call(
        _merge_kernel,
        grid=(t // tm,),
        in_specs=[rowblk(RWKV_DIM), rowblk(SG_DIM), rowblk(ATT_DIM), rowblk(GATE_COLS), const(1, GATE_COLS),
                  const(d, d), const(d, d), rowblk(d), const(1, d), const(1, d)],
        out_specs=rowblk(d),
        out_shape=jax.ShapeDtypeStruct((t, d), F32),
        compiler_params=_cparams("parallel"),
        name="merge",
    )(y_rwkv, y_sg, y_att, p_gate, b_gate, w_branch, w_o, x, g, b)


def _rel_bucket(dist):
    max_exact = REL_BUCKETS // 2
    d_f = jnp.maximum(dist, 1).astype(F32)
    large = max_exact + (jnp.log(d_f / max_exact) / math.log(REL_MAX_DIST / max_exact)
                         * (REL_BUCKETS - max_exact)).astype(I32)
    large = jnp.minimum(large, REL_BUCKETS - 1)
    return jnp.where(dist < max_exact, dist, large)


def _bias_tables(rel_bias):
    n_ap = ATT_HEADS // 2
    rk = jnp.arange(Q_BLOCK, dtype=I32)[:, None]
    lq = jnp.arange(2 * Q_BLOCK, dtype=I32)[None, :]
    near = []
    for delta in (0, 1):
        dist = delta * Q_BLOCK + (lq % Q_BLOCK) - rk
        bucket = _rel_bucket(jnp.maximum(dist, 0))
        per_pair = []
        for hp in range(n_ap):
            head = 2 * hp + (lq // Q_BLOCK)
            per_pair.append(rel_bias[bucket, jnp.broadcast_to(head, bucket.shape)])
        near.append(jnp.stack(per_pair))
    near = jnp.stack(near)
    far_bucket = _rel_bucket(jnp.full((), 2 * Q_BLOCK, I32))
    far = jnp.repeat(rel_bias[far_bucket].reshape(n_ap, 2), Q_BLOCK, axis=1)[:, None, :]
    return near.astype(F32), far.astype(F32)


def _pad_rows(w, rows):
    return jnp.concatenate([w, jnp.zeros((rows - w.shape[0], w.shape[1]), w.dtype)], axis=0)


def kernel(x, ffn1_w_in, ffn1_w_out, ln1_g, ln1_b, w_in, b_gate, rwkv_mu, rwkv_w0, rwkv_w2, rwkv_a0, rwkv_a2, rwkv_g2, rwkv_k_k, rwkv_k_a, rwkv_r_k, rwkv_gn_g, rwkv_gn_b, sg_ln_g, sg_ln_b, sg_w, sg_b, idx_ln_g, idx_ln_b, rel_bias, w_branch, w_o, ln2_g, ln2_b, ffn2_w_in, ffn2_w_out, ln3_g, ln3_b):
    batch, seq, d = x.shape
    t = batch * seq
    nb = seq // Q_BLOCK
    depth = w_in.shape[0]
    xf = x.reshape(t, d)
    bias_near, bias_far = _bias_tables(rel_bias)
    row2 = lambda v: v.reshape(1, -1)

    for l in range(depth):
        xf, xb = _ffn_ln(xf, ffn1_w_in[l].astype(BF16), ffn1_w_out[l].astype(BF16), row2(ln1_g[l]), row2(ln1_b[l]))

        w = w_in[l]
        c0 = RWKV_COLS
        c1 = c0 + SG_COLS
        c2 = c1 + ATT_COLS
        w_rwkv = jnp.concatenate([w[:, :c0], jnp.zeros((d, RWKV_COLS_PAD - RWKV_COLS), w.dtype)], axis=1).astype(BF16)
        w_sg = w[:, c0:c1].astype(BF16)
        wa = w[:, c1:c2]
        qkv_w = 3 * ATT_DIM
        w_attm = jnp.concatenate([wa[:, qkv_w:qkv_w + IDX_HEADS * IDX_HEAD_DIM], wa[:, :qkv_w]], axis=1).astype(BF16)
        n_misc = IDX_HEAD_DIM + IDX_HEADS
        w_attx = jnp.concatenate([wa[:, ATT_COLS - n_misc:], jnp.zeros((d, LANES - n_misc), w.dtype)], axis=1).astype(BF16)
        w_gate = w[:, c2:].astype(BF16)

        p_rwkv = _matmul(xb, w_rwkv, F32, bm=1024, bn=1152, name="proj_rwkv")
        p_sg = _matmul(xb, w_sg, F32, bm=1024, bn=1024, name="proj_sg")
        p_attm = _matmul(xb, w_attm, BF16, bm=1024, bn=1280, name="proj_att")
        p_attx = _matmul(xb, w_attx, F32, bm=1024, bn=LANES, name="proj_attx")
        p_gate = _matmul(xb, w_gate, F32, bm=1024, bn=1024, name="proj_gate")

        mu_pad = jnp.concatenate([rwkv_mu[l], jnp.zeros((RWKV_COLS_PAD - RWKV_COLS,), F32)]).reshape(1, -1)
        w2p = _pad_rows(rwkv_w2[l], LANES).astype(BF16)
        a2p = jnp.concatenate([jnp.zeros((DECAY_LORA, RWKV_DIM), F32), rwkv_a2[l]], axis=0).astype(BF16)
        g2p = _pad_rows(rwkv_g2[l], 2 * LANES).astype(BF16)
        y_rwkv = _rwkv(p_rwkv, mu_pad, row2(rwkv_w0[l]), row2(rwkv_a0[l]), row2(rwkv_k_k[l]), row2(rwkv_k_a[l]),
                       row2(rwkv_r_k[l]), row2(rwkv_gn_g[l]), row2(rwkv_gn_b[l]), w2p, a2p, g2p,
                       batch=batch, seq=seq)

        sg_bias = jnp.repeat(sg_b[l].T, SG_GROUP_DIM, axis=1)
        y_sg = _sgu(p_sg, row2(sg_ln_g[l]), row2(sg_ln_b[l]), sg_w[l], sg_bias)

        kidx2 = _kidx_ln(p_attx, row2(jnp.tile(idx_ln_g[l], 2)), row2(jnp.tile(idx_ln_b[l], 2)))
        w_t = p_attx[:, IDX_HEAD_DIM:IDX_HEAD_DIM + IDX_HEADS].T
        v_t = p_attm[:, ATT_MAIN_COLS - ATT_DIM:].reshape(batch, nb, Q_BLOCK, ATT_DIM).transpose(0, 1, 3, 2)
        y_att = _dsa(p_attm, v_t, kidx2, w_t, bias_near, bias_far, batch=batch, seq=seq)

        xf = _merge(y_rwkv, y_sg, y_att, p_gate, row2(b_gate[l]), w_branch[l].astype(BF16), w_o[l].astype(BF16),
                    xf, row2(ln2_g[l]), row2(ln2_b[l]))

        xf, _ = _ffn_ln(xf, ffn2_w_in[l].astype(BF16), ffn2_w_out[l].astype(BF16), row2(ln3_g[l]), row2(ln3_b[l]))

    return xf.reshape(batch, seq, d)
```

```python
import functools
import math

import jax
import jax.numpy as jnp
from jax import lax
from jax.experimental import pallas as pl
from jax.experimental.pallas import tpu as pltpu

F32 = jnp.float32
BF16 = jnp.bfloat16
I32 = jnp.int32

V7X_VMEM_BYTES = 64 * 1024 * 1024
VMEM_LIMIT_BYTES = V7X_VMEM_BYTES - 8 * 1024 * 1024
LANES = 128

D_MODEL = 2048
DEPTH = 4
RWKV_HEADS, RWKV_HEAD_DIM = 16, 64
RWKV_DIM = RWKV_HEADS * RWKV_HEAD_DIM
DECAY_LORA, AAA_LORA, GATE_LORA = 64, 64, 160
RWKV_GN_EPS = 64e-5
SG_GROUPS, SG_GROUP_DIM, SG_CHUNK = 4, 128, 128
SG_DIM = SG_GROUPS * SG_GROUP_DIM
ATT_HEADS, ATT_HEAD_DIM = 8, 64
ATT_DIM = ATT_HEADS * ATT_HEAD_DIM
IDX_HEADS, IDX_HEAD_DIM = 16, 64
TOPK_MAX = 256
Q_BLOCK = 128
REL_BUCKETS, REL_MAX_DIST = 32, 128
D_FF = 5632
LN_EPS = 1e-5
DEEPNORM_ALPHA = (2 * DEPTH) ** 0.25

RWKV_COLS = 3 * RWKV_DIM + DECAY_LORA + AAA_LORA + GATE_LORA
RWKV_LORA_PAD = 384
RWKV_COLS_PAD = 3 * RWKV_DIM + RWKV_LORA_PAD
SG_COLS = 2 * SG_DIM
ATT_MAIN_COLS = IDX_HEADS * IDX_HEAD_DIM + 3 * ATT_DIM
ATT_COLS = 3 * ATT_DIM + IDX_HEADS * IDX_HEAD_DIM + IDX_HEAD_DIM + IDX_HEADS
GATE_COLS = 3 * D_MODEL

RWKV_CHUNK = 64
DSA_SUPER = 4
NEG_BIG = -1e30


def _cparams(*sem):
    return pltpu.CompilerParams(dimension_semantics=sem, vmem_limit_bytes=VMEM_LIMIT_BYTES)


def _layer_norm(y, g, b, eps):
    mu = jnp.mean(y, axis=-1, keepdims=True)
    d = y - mu
    var = jnp.mean(d * d, axis=-1, keepdims=True)
    return d * lax.rsqrt(var + eps) * g + b


def _dot(a, b):
    return jnp.dot(a, b, preferred_element_type=F32)


def _dot_nt(a, b):
    return lax.dot_general(a, b, (((1,), (1,)), ((), ())), preferred_element_type=F32)


def _ffn_kernel(x_ref, wg_ref, wu_ref, wo_ref, g_ref, b_ref, o_ref, ob_ref, xb_ref, acc_ref):
    j = pl.program_id(1)

    @pl.when(j == 0)
    def _():
        xb_ref[...] = x_ref[...].astype(BF16)
        acc_ref[...] = jnp.zeros_like(acc_ref)

    xb = xb_ref[...]
    gate = _dot(xb, wg_ref[...])
    up = _dot(xb, wu_ref[...])
    h = (gate * jax.nn.sigmoid(gate) * up).astype(BF16)
    acc_ref[...] += _dot(h, wo_ref[...])

    @pl.when(j == pl.num_programs(1) - 1)
    def _():
        y = DEEPNORM_ALPHA * x_ref[...] + 0.5 * acc_ref[...]
        out = _layer_norm(y, g_ref[...], b_ref[...], LN_EPS)
        o_ref[...] = out
        ob_ref[...] = out.astype(BF16)


def _ffn_ln(x, w_in, w_out, g, b, *, tm=512, tf=512):
    t, d = x.shape
    f = w_out.shape[0]
    nj = f // tf
    return pl.pallas_call(
        _ffn_kernel,
        grid=(t // tm, nj),
        in_specs=[
            pl.BlockSpec((tm, d), lambda i, j: (i, 0)),
            pl.BlockSpec((d, tf), lambda i, j: (0, j)),
            pl.BlockSpec((d, tf), lambda i, j: (0, j + nj)),
            pl.BlockSpec((tf, d), lambda i, j: (j, 0)),
            pl.BlockSpec((1, d), lambda i, j: (0, 0)),
            pl.BlockSpec((1, d), lambda i, j: (0, 0)),
        ],
        out_specs=[
            pl.BlockSpec((tm, d), lambda i, j: (i, 0)),
            pl.BlockSpec((tm, d), lambda i, j: (i, 0)),
        ],
        out_shape=[jax.ShapeDtypeStruct((t, d), F32), jax.ShapeDtypeStruct((t, d), BF16)],
        scratch_shapes=[pltpu.VMEM((tm, d), BF16), pltpu.VMEM((tm, d), F32)],
        compiler_params=_cparams("parallel", "arbitrary"),
        name="ffn_ln",
    )(x, w_in, w_in, w_out, g, b)


def _mm_kernel(x_ref, w_ref, o_ref):
    o_ref[...] = _dot(x_ref[...], w_ref[...]).astype(o_ref.dtype)


def _matmul(x, w, out_dtype, *, bm, bn, name):
    t, k = x.shape
    n = w.shape[1]
    return pl.pallas_call(
        _mm_kernel,
        grid=(t // bm, n // bn),
        in_specs=[pl.BlockSpec((bm, k), lambda i, j: (i, 0)), pl.BlockSpec((k, bn), lambda i, j: (0, j))],
        out_specs=pl.BlockSpec((bm, bn), lambda i, j: (i, j)),
        out_shape=jax.ShapeDtypeStruct((t, n), out_dtype),
        compiler_params=_cparams("parallel", "arbitrary"),
        name=name,
    )(x, w)


def _split3(x):
    h = x.astype(BF16)
    r1 = x - h.astype(F32)
    m = r1.astype(BF16)
    lo = (r1 - m.astype(F32)).astype(BF16)
    return h, m, lo


def _dot3(x, w_bf16):
    h, m, lo = _split3(x)
    return _dot(h, w_bf16) + _dot(m, w_bf16) + _dot(lo, w_bf16)


def _rwkv_kernel(r_ref, k_ref, v_ref, lo_ref, mur_ref, muk_ref, muv_ref, mulo_ref,
                 w0_ref, a0_ref, kk_ref, ka_ref, rk_ref, gg_ref, gb_ref,
                 w2_ref, a2_ref, g2_ref, o_ref,
                 s_ref, pr_ref, pk_ref, pv_ref, plo_ref):
    tb = pl.program_id(2)
    ts = r_ref.shape[0]
    C = RWKV_CHUNK
    H2 = 2 * C
    nch = ts // C

    @pl.when(tb == 0)
    def _():
        s_ref[...] = jnp.zeros_like(s_ref)
        pr_ref[...] = jnp.zeros_like(pr_ref)
        pk_ref[...] = jnp.zeros_like(pk_ref)
        pv_ref[...] = jnp.zeros_like(pv_ref)
        plo_ref[...] = jnp.zeros_like(plo_ref)

    def shift(x_ref, prev_ref, mu_ref):
        x = x_ref[...]
        first = lax.broadcasted_iota(I32, x.shape, 0) == 0
        xp = jnp.where(first, prev_ref[...], pltpu.roll(x, 1, 0))
        prev_ref[...] = x[ts - 1:ts, :]
        return x + (xp - x) * mu_ref[...]

    r = shift(r_ref, pr_ref, mur_ref)
    k = shift(k_ref, pk_ref, muk_ref)
    v = shift(v_ref, pv_ref, muv_ref)
    lo = shift(lo_ref, plo_ref, mulo_ref)

    lane = lax.broadcasted_iota(I32, (LANES, LANES), 1)
    row = lax.broadcasted_iota(I32, (LANES, LANES), 0)
    same_head = (lane < 64) == (row < 64)
    seg_ones = jnp.where(same_head, 1.0, 0.0).astype(BF16)

    wa = lo[:, :LANES]
    zw = w0_ref[...] + _dot(jnp.tanh(wa).astype(BF16), w2_ref[...])
    nz = -zw
    softplus = jnp.maximum(nz, 0.0) + jnp.log1p(jnp.exp(-jnp.abs(nz)))
    logw = -jnp.exp(-softplus - 0.5)
    a = jax.nn.sigmoid(a0_ref[...] + _dot(wa.astype(BF16), a2_ref[...]))
    g = _dot(jax.nn.sigmoid(lo[:, LANES:]).astype(BF16), g2_ref[...])

    kk = k * kk_ref[...]
    n2 = _dot3(kk * kk, seg_ones)
    kkn = kk / jnp.maximum(jnp.sqrt(n2), 1e-12)
    k2 = k * (1.0 + (a - 1.0) * ka_ref[...])

    rin = lax.broadcasted_iota(I32, (ts, LANES), 0) & (C - 1)
    cum = logw
    step = 1
    while step < C:
        cum = cum + jnp.where(rin >= step, pltpu.roll(cum, step, 0), 0.0)
        step *= 2

    aa = -kkn
    bb = kkn * a
    p_inv = jnp.exp(-cum)
    at = aa * jnp.exp(cum - logw)
    rt = r * jnp.exp(cum)
    kt = k2 * p_inv
    bt = bb * p_inv

    head_a = lax.broadcasted_iota(I32, (ts, LANES), 1) < 64
    head_a_c = lax.broadcasted_iota(I32, (C, LANES), 1) < 64

    def by_head(x):
        xb = x.astype(BF16)
        z = jnp.zeros_like(xb)
        return jnp.where(head_a, xb, z), jnp.where(head_a, z, xb)

    def rows(x, c):
        return x[c * C:(c + 1) * C, :]

    def stack(pair, c):
        return jnp.concatenate([rows(pair[0], c), rows(pair[1], c)], axis=0)

    def stack_val(x):
        xb = x.astype(BF16)
        z = jnp.zeros_like(xb)
        return jnp.concatenate([jnp.where(head_a_c, xb, z), jnp.where(head_a_c, z, xb)], axis=0)

    def bdiag(x):
        xb = x.astype(BF16)
        return jnp.where(same_head, jnp.concatenate([xb, xb], axis=0), jnp.zeros((H2, H2), BF16))

    crow = lax.broadcasted_iota(I32, (C, H2), 0)
    ccol = lax.broadcasted_iota(I32, (C, H2), 1)
    ccol = jnp.where(ccol >= C, ccol - C, ccol)
    strict = ccol < crow
    incl = ccol <= crow
    eye_rp = jnp.where(ccol == crow, 1.0, 0.0)
    eye128 = lane == row

    bt_h = by_head(bt)
    kt_h = by_head(kt)
    at_h = by_head(at)
    v_h = by_head(v)
    at_b = at.astype(BF16)
    rt_b = rt.astype(BF16)
    ch = range(nch)

    ar = [jnp.concatenate([rows(at_b, c), rows(rt_b, c)], axis=0) for c in ch]
    xb = [_dot_nt(ar[c], stack(bt_h, c)) for c in ch]
    xk = [_dot_nt(ar[c], stack(kt_h, c)) for c in ch]
    a_ab = [jnp.where(strict, xb[c][:C], 0.0) for c in ch]
    a_rb = [jnp.where(incl, xb[c][C:], 0.0).astype(BF16) for c in ch]
    a_ak = [jnp.where(strict, xk[c][:C], 0.0).astype(BF16) for c in ch]
    a_rk = [jnp.where(incl, xk[c][C:], 0.0).astype(BF16) for c in ch]

    tinv = [eye_rp + a_ab[c] for c in ch]
    apow = a_ab
    n = 1
    while n < C // 2:
        apow = [_dot(apow[c].astype(BF16), bdiag(apow[c])) for c in ch]
        tinv = [tinv[c] + _dot(tinv[c].astype(BF16), bdiag(apow[c])) for c in ch]
        n *= 2
    tinv = [tinv[c].astype(BF16) for c in ch]

    gv = [_dot(a_ak[c], stack(v_h, c)) for c in ch]
    w1 = [_dot(tinv[c], stack(at_h, c)) for c in ch]
    w2 = [_dot(tinv[c], stack_val(gv[c])) for c in ch]
    rq = [(rows(rt, c) + _dot(a_rb[c], stack_val(w1[c]))).astype(BF16) for c in ch]
    y0 = [_dot(a_rk[c], stack(v_h, c)) + _dot(a_rb[c], stack_val(w2[c])) for c in ch]

    cl = [cum[c * C + C - 1:(c + 1) * C, :] for c in ch]
    p_end = [jnp.exp(cl[c] - rows(cum, c)) for c in ch]
    kbt = [jnp.concatenate([rows(k2, c) * p_end[c], rows(bb, c) * p_end[c]], axis=0).T.astype(BF16) for c in ch]
    v_b = v.astype(BF16)
    n_add = [jnp.where(same_head, _dot(kbt[c], jnp.concatenate([rows(v_b, c), w2[c].astype(BF16)], axis=0)), 0.0)
             for c in ch]
    zeros_c = jnp.zeros((C, LANES), BF16)
    m_mat = [(jnp.where(same_head, _dot(kbt[c], jnp.concatenate([zeros_c, w1[c].astype(BF16)], axis=0)), 0.0)
              + jnp.where(eye128, jnp.exp(cl[c]), 0.0)).astype(BF16) for c in ch]

    s_cur = s_ref[...]
    ys = []
    for c in ch:
        s_b = s_cur.astype(BF16)
        ys.append(_dot(rq[c], s_b) + y0[c])
        s_cur = _dot(m_mat[c], s_b) + n_add[c]
    s_ref[...] = s_cur
    y = jnp.concatenate(ys, axis=0)

    inv_n = 1.0 / RWKV_HEAD_DIM
    mu = _dot3(y, seg_ones) * inv_n
    d = y - mu
    var = _dot3(d * d, seg_ones) * inv_n
    yn = d * lax.rsqrt(var + RWKV_GN_EPS) * gg_ref[...] + gb_ref[...]
    bonus = _dot3(r * k2 * rk_ref[...], seg_ones) * v
    o_ref[...] = ((yn + bonus) * g).astype(o_ref.dtype)


def _rwkv(p, mu_pad, w0, a0, k_k, k_a, r_k, gn_g, gn_b, w2p, a2p, g2p, *, batch, seq, ts=512):
    t = p.shape[0]
    npair = RWKV_HEADS // 2
    nt = seq // ts
    nlora = 3 * RWKV_DIM // RWKV_LORA_PAD

    def col(off):
        return pl.BlockSpec((ts, LANES), lambda b, h, s, off=off: (b * nt + s, off + h))

    def prow(off=0):
        return pl.BlockSpec((1, LANES), lambda b, h, s, off=off: (0, off + h))

    f32 = lambda *shape: pltpu.VMEM(shape, F32)
    return pl.pallas_call(
        _rwkv_kernel,
        grid=(batch, npair, nt),
        in_specs=[
            col(0), col(npair), col(2 * npair),
            pl.BlockSpec((ts, RWKV_LORA_PAD), lambda b, h, s: (b * nt + s, nlora)),
            prow(0), prow(npair), prow(2 * npair),
            pl.BlockSpec((1, RWKV_LORA_PAD), lambda b, h, s: (0, nlora)),
            prow(), prow(), prow(), prow(), prow(), prow(), prow(),
            pl.BlockSpec((LANES, LANES), lambda b, h, s: (0, h)),
            pl.BlockSpec((LANES, LANES), lambda b, h, s: (0, h)),
            pl.BlockSpec((2 * LANES, LANES), lambda b, h, s: (0, h)),
        ],
        out_specs=pl.BlockSpec((ts, LANES), lambda b, h, s: (b * nt + s, h)),
        out_shape=jax.ShapeDtypeStruct((t, RWKV_DIM), BF16),
        scratch_shapes=[
            f32(LANES, LANES), f32(1, LANES), f32(1, LANES), f32(1, LANES), f32(1, RWKV_LORA_PAD),
        ],
        compiler_params=_cparams("parallel", "parallel", "arbitrary"),
        name="rwkv7",
    )(p, p, p, p, mu_pad, mu_pad, mu_pad, mu_pad, w0, a0, k_k, k_a, r_k, gn_g, gn_b, w2p, a2p, g2p)


def _sgu_kernel(p_ref, lg_ref, lb_ref, w_ref, bs_ref, o_ref):
    tm = p_ref.shape[0]
    p = p_ref[...]
    z = 0.5 * p * (1.0 + lax.erf(p * math.sqrt(0.5)))
    u = z[:, :SG_DIM]
    v = _layer_norm(z[:, SG_DIM:], lg_ref[...], lb_ref[...], LN_EPS).astype(BF16)
    rowi = lax.broadcasted_iota(I32, (SG_CHUNK, SG_CHUNK), 0)
    colj = lax.broadcasted_iota(I32, (SG_CHUNK, SG_CHUNK), 1)
    causal = colj <= rowi
    for g in range(SG_GROUPS):
        wg = jnp.where(causal, w_ref[g], 0.0).astype(BF16)
        gs = slice(g * SG_GROUP_DIM, (g + 1) * SG_GROUP_DIM)
        for c in range(tm // SG_CHUNK):
            rs = slice(c * SG_CHUNK, (c + 1) * SG_CHUNK)
            mixed = _dot(wg, v[rs, gs]) + bs_ref[:, gs]
            o_ref[rs, gs] = (u[rs, gs] * mixed).astype(o_ref.dtype)


def _sgu(p, ln_g, ln_b, w_s, bias_full, *, tm=512):
    t = p.shape[0]
    return pl.pallas_call(
        _sgu_kernel,
        grid=(t // tm,),
        in_specs=[
            pl.BlockSpec((tm, SG_COLS), lambda i: (i, 0)),
            pl.BlockSpec((1, SG_DIM), lambda i: (0, 0)),
            pl.BlockSpec((1, SG_DIM), lambda i: (0, 0)),
            pl.BlockSpec((SG_GROUPS, SG_CHUNK, SG_CHUNK), lambda i: (0, 0, 0)),
            pl.BlockSpec((SG_CHUNK, SG_DIM), lambda i: (0, 0)),
        ],
        out_specs=pl.BlockSpec((tm, SG_DIM), lambda i: (i, 0)),
        out_shape=jax.ShapeDtypeStruct((t, SG_DIM), BF16),
        compiler_params=_cparams("parallel"),
        name="sgu",
    )(p, ln_g, ln_b, w_s, bias_full)


def _kidx_kernel(x_ref, g_ref, b_ref, o_ref):
    x = x_ref[...]
    lane = lax.broadcasted_iota(I32, x.shape, 1)
    x2 = jnp.where(lane < IDX_HEAD_DIM, x, pltpu.roll(x, IDX_HEAD_DIM, 1))
    o_ref[...] = _layer_norm(x2, g_ref[...], b_ref[...], LN_EPS).astype(o_ref.dtype)


def _kidx_ln(x, g2, b2, *, tm=2048):
    t = x.shape[0]
    return pl.pallas_call(
        _kidx_kernel,
        grid=(t // tm,),
        in_specs=[pl.BlockSpec((tm, LANES), lambda i: (i, 0)),
                  pl.BlockSpec((1, LANES), lambda i: (0, 0)),
                  pl.BlockSpec((1, LANES), lambda i: (0, 0))],
        out_specs=pl.BlockSpec((tm, LANES), lambda i: (i, 0)),
        out_shape=jax.ShapeDtypeStruct((t, LANES), BF16),
        compiler_params=_cparams("parallel"),
        name="kidx_ln",
    )(x, g2, b2)


def _dsa_kernel(qi_ref, q_ref, k_ref, vt_ref, kx_ref, wt_ref, b3_ref, o_ref,
                rt_ref, qt_ref, key_ref, m_ref, l_ref, acc_ref, *, topk):
    i = pl.program_id(1)
    QB = Q_BLOCK
    head_a = lax.broadcasted_iota(I32, (QB, LANES), 1) < 64

    def pair_rhs(x):
        zero = jnp.zeros_like(x)
        return jnp.concatenate([jnp.where(head_a, x, zero), jnp.where(head_a, zero, x)], axis=0)

    n_ip = IDX_HEADS // 2
    n_ap = ATT_HEADS // 2
    for hp in range(n_ip):
        rt_ref[hp] = pair_rhs(qi_ref[:, hp * LANES:(hp + 1) * LANES])
    for hp in range(n_ap):
        qt_ref[hp] = pair_rhs(q_ref[:, hp * LANES:(hp + 1) * LANES])

    scale = (IDX_HEAD_DIM ** -0.5) * (IDX_HEADS ** -0.5)
    wt = wt_ref[...] * scale
    wrows = [jnp.concatenate([wt[2 * hp:2 * hp + 1, :], wt[2 * hp + 1:2 * hp + 2, :]], axis=1)
             for hp in range(n_ip)]

    int_min = jnp.int32(-2 ** 31)

    SK = DSA_SUPER * QB
    n_super = (i + DSA_SUPER) // DSA_SUPER
    krow = lax.broadcasted_iota(I32, (SK, LANES), 0)
    qlane = lax.broadcasted_iota(I32, (SK, LANES), 1)

    def score_step(sc_i, carry):
        ks = pl.ds(pl.multiple_of(sc_i * SK, SK), SK)
        kc = kx_ref[ks, :]
        acc = jnp.maximum(_dot_nt(kc, rt_ref[0]), 0.0) * wrows[0]
        for hp in range(1, n_ip):
            acc = acc + jnp.maximum(_dot_nt(kc, rt_ref[hp]), 0.0) * wrows[hp]
        sc = acc[:, :QB] + acc[:, QB:]
        bits = lax.bitcast_convert_type(sc, I32)
        key = bits ^ ((bits >> 31) & jnp.int32(0x7FFFFFFF))
        valid = (sc_i * SK + krow) <= (i * QB + qlane)
        key_ref[ks, :] = jnp.where(valid, key, int_min)
        return carry

    lax.fori_loop(0, n_super, score_step, 0)

    def bit_step(bi, t):
        cand = t + lax.shift_left(jnp.int32(1), 31 - bi)

        def count_step(sc_i, cnt):
            ks = pl.ds(pl.multiple_of(sc_i * SK, SK), SK)
            ge = jnp.where(key_ref[ks, :] >= cand, 1, 0)
            parts = [ge[r8 * 8:(r8 + 1) * 8, :] for r8 in range(SK // 8)]
            while len(parts) > 1:
                parts = [parts[j] + parts[j + 1] for j in range(0, len(parts), 2)]
            return cnt + parts[0]

        cnt8 = lax.fori_loop(0, n_super, count_step, jnp.zeros((8, LANES), I32))
        cnt = jnp.sum(cnt8, axis=0, keepdims=True)
        return jnp.where(cnt >= topk, cand, t)

    thr = lax.fori_loop(0, 32, bit_step, jnp.full((1, LANES), int_min, I32))

    thr = jnp.maximum(thr, int_min + 1)
    m_ref[...] = jnp.full(m_ref.shape, NEG_BIG, F32)
    l_ref[...] = jnp.zeros(l_ref.shape, F32)
    acc_ref[...] = jnp.zeros(acc_ref.shape, F32)
    att_scale = ATT_HEAD_DIM ** -0.5

    def attend_step(sc_i, carry):
        base = sc_i * DSA_SUPER
        ks = pl.ds(pl.multiple_of(sc_i * SK, SK), SK)
        madd = jnp.where(key_ref[ks, :] >= thr, 0.0, 2.0 * NEG_BIG)
        madd2 = jnp.concatenate([madd, madd], axis=1)
        raw = [_dot_nt(k_ref[ks, hp * LANES:(hp + 1) * LANES], qt_ref[hp]) for hp in range(n_ap)]
        kinds = [jnp.clip(i - (base + j), 0, 2) for j in range(DSA_SUPER)]
        ps = []
        for hp in range(n_ap):
            bias = jnp.concatenate([b3_ref[kinds[j], hp] for j in range(DSA_SUPER)], axis=0)
            s = raw[hp] * att_scale + bias + madd2
            m_old = m_ref[hp]
            m_new = jnp.maximum(m_old, jnp.max(s, axis=0, keepdims=True))
            p = jnp.exp(s - m_new)
            alpha = jnp.exp(m_old - m_new)
            l_ref[hp] = alpha * l_ref[hp] + jnp.sum(p, axis=0, keepdims=True)
            m_ref[hp] = m_new
            ps.append((p.astype(BF16), alpha))
        for hp in range(n_ap):
            vt = jnp.concatenate([vt_ref[0, base + j, hp * LANES:(hp + 1) * LANES, :] for j in range(DSA_SUPER)],
                                 axis=1)
            acc_ref[hp] = acc_ref[hp] * ps[hp][1] + _dot(vt, ps[hp][0])
        return carry

    lax.fori_loop(0, n_super, attend_step, 0)

    for hp in range(n_ap):
        o2 = acc_ref[hp] * (1.0 / l_ref[hp])
        pair_t = jnp.concatenate([o2[:64, :QB], o2[64:, QB:]], axis=0)
        o_ref[:, hp * LANES:(hp + 1) * LANES] = pair_t.T.astype(o_ref.dtype)


def _dsa(p_main, vt, kidx2, w_t, bias3, *, batch, seq):
    t = p_main.shape[0]
    nb = seq // Q_BLOCK
    assert nb % DSA_SUPER == 0
    top_k = min(TOPK_MAX, seq // 4)
    n_ip, n_ap = IDX_HEADS // 2, ATT_HEADS // 2
    qi_w = IDX_HEADS * IDX_HEAD_DIM
    return pl.pallas_call(
        functools.partial(_dsa_kernel, topk=top_k),
        grid=(batch, nb),
        in_specs=[
            pl.BlockSpec((Q_BLOCK, qi_w), lambda b, i: (b * nb + i, 0)),
            pl.BlockSpec((Q_BLOCK, ATT_DIM), lambda b, i: (b * nb + i, qi_w // ATT_DIM)),
            pl.BlockSpec((seq, ATT_DIM), lambda b, i: (b, qi_w // ATT_DIM + 1)),
            pl.BlockSpec((1, nb, ATT_DIM, Q_BLOCK), lambda b, i: (b, 0, 0, 0)),
            pl.BlockSpec((seq, LANES), lambda b, i: (b, 0)),
            pl.BlockSpec((IDX_HEADS, Q_BLOCK), lambda b, i: (0, b * nb + i)),
            pl.BlockSpec((3, n_ap, Q_BLOCK, 2 * Q_BLOCK), lambda b, i: (0, 0, 0, 0)),
        ],
        out_specs=pl.BlockSpec((Q_BLOCK, ATT_DIM), lambda b, i: (b * nb + i, 0)),
        out_shape=jax.ShapeDtypeStruct((t, ATT_DIM), BF16),
        scratch_shapes=[
            pltpu.VMEM((n_ip, 2 * Q_BLOCK, LANES), BF16),
            pltpu.VMEM((n_ap, 2 * Q_BLOCK, LANES), BF16),
            pltpu.VMEM((seq, Q_BLOCK), I32),
            pltpu.VMEM((n_ap, 1, 2 * Q_BLOCK), F32),
            pltpu.VMEM((n_ap, 1, 2 * Q_BLOCK), F32),
            pltpu.VMEM((n_ap, LANES, 2 * Q_BLOCK), F32),
        ],
        compiler_params=_cparams("parallel", "arbitrary"),
        name="dsa",
    )(p_main, p_main, p_main, vt, kidx2, w_t, bias3)


def _merge_kernel(yr_ref, ys_ref, ya_ref, pg_ref, bg_ref, wb_ref, wo_ref, x_ref, g_ref, b_ref, o_ref):
    d = D_MODEL
    z = _dot(yr_ref[...], wb_ref[0:RWKV_DIM, :])
    merged = jax.nn.sigmoid(pg_ref[:, 0:d] + bg_ref[:, 0:d]) * z
    z = _dot(ys_ref[...], wb_ref[RWKV_DIM:RWKV_DIM + SG_DIM, :])
    merged = merged + jax.nn.sigmoid(pg_ref[:, d:2 * d] + bg_ref[:, d:2 * d]) * z
    z = _dot(ya_ref[...], wb_ref[RWKV_DIM + SG_DIM:, :])
    merged = merged + jax.nn.sigmoid(pg_ref[:, 2 * d:] + bg_ref[:, 2 * d:]) * z
    mix = _dot(merged.astype(BF16), wo_ref[...])
    y = DEEPNORM_ALPHA * x_ref[...] + mix
    o_ref[...] = _layer_norm(y, g_ref[...], b_ref[...], LN_EPS)


def _merge(y_rwkv, y_sg, y_att, p_gate, b_gate, w_branch, w_o, x, g, b, *, tm=256):
    t, d = x.shape
    rowblk = lambda w: pl.BlockSpec((tm, w), lambda i: (i, 0))
    const = lambda r, c: pl.BlockSpec((r, c), lambda i: (0, 0))
    return pl.pallas_call(
        _merge_kernel,
        grid=(t // tm,),
        in_specs=[rowblk(RWKV_DIM), rowblk(SG_DIM), rowblk(ATT_DIM), rowblk(GATE_COLS), const(1, GATE_COLS),
                  const(d, d), const(d, d), rowblk(d), const(1, d), const(1, d)],
        out_specs=rowblk(d),
        out_shape=jax.ShapeDtypeStruct((t, d), F32),
        compiler_params=_cparams("parallel"),
        name="merge",
    )(y_rwkv, y_sg, y_att, p_gate, b_gate, w_branch, w_o, x, g, b)


def _rel_bucket(dist):
    max_exact = REL_BUCKETS // 2
    d_f = jnp.maximum(dist, 1).astype(F32)
    large = max_exact + (jnp.log(d_f / max_exact) / math.log(REL_MAX_DIST / max_exact)
                         * (REL_BUCKETS - max_exact)).astype(I32)
    large = jnp.minimum(large, REL_BUCKETS - 1)
    return jnp.where(dist < max_exact, dist, large)


def _bias_tables(rel_bias):
    n_ap = ATT_HEADS // 2
    rk = jnp.arange(Q_BLOCK, dtype=I32)[:, None]
    lq = jnp.arange(2 * Q_BLOCK, dtype=I32)[None, :] % Q_BLOCK
    per_pair = [jnp.concatenate([jnp.broadcast_to(rel_bias[:, 2 * hp, None], (REL_BUCKETS, Q_BLOCK)),
                                 jnp.broadcast_to(rel_bias[:, 2 * hp + 1, None], (REL_BUCKETS, Q_BLOCK))], axis=1)
                for hp in range(n_ap)]
    tiles = []
    for delta in (0, 1, 2):
        bucket = _rel_bucket(jnp.maximum(delta * Q_BLOCK + lq - rk, 0))
        pairs = []
        for hp in range(n_ap):
            t = jnp.zeros((Q_BLOCK, 2 * Q_BLOCK), F32)
            for bkt in range(REL_BUCKETS):
                t = t + jnp.where(bucket == bkt, per_pair[hp][bkt][None, :], 0.0)
            pairs.append(t)
        tiles.append(jnp.stack(pairs))
    return jnp.stack(tiles)


def _pad_rows(w, rows):
    return jnp.concatenate([w, jnp.zeros((rows - w.shape[0], w.shape[1]), w.dtype)], axis=0)


def kernel(x, ffn1_w_in, ffn1_w_out, ln1_g, ln1_b, w_in, b_gate, rwkv_mu, rwkv_w0, rwkv_w2, rwkv_a0, rwkv_a2, rwkv_g2, rwkv_k_k, rwkv_k_a, rwkv_r_k, rwkv_gn_g, rwkv_gn_b, sg_ln_g, sg_ln_b, sg_w, sg_b, idx_ln_g, idx_ln_b, rel_bias, w_branch, w_o, ln2_g, ln2_b, ffn2_w_in, ffn2_w_out, ln3_g, ln3_b):
    batch, seq, d = x.shape
    t = batch * seq
    nb = seq // Q_BLOCK
    depth = w_in.shape[0]
    xf = x.reshape(t, d)
    bias3 = _bias_tables(rel_bias)
    row2 = lambda v: v.reshape(1, -1)

    for l in range(depth):
        xf, xb = _ffn_ln(xf, ffn1_w_in[l].astype(BF16), ffn1_w_out[l].astype(BF16), row2(ln1_g[l]), row2(ln1_b[l]))

        w = w_in[l]
        c0 = RWKV_COLS
        c1 = c0 + SG_COLS
        c2 = c1 + ATT_COLS
        w_rwkv = jnp.concatenate([w[:, :c0], jnp.zeros((d, RWKV_COLS_PAD - RWKV_COLS), w.dtype)], axis=1).astype(BF16)
        w_sg = w[:, c0:c1].astype(BF16)
        wa = w[:, c1:c2]
        qkv_w = 3 * ATT_DIM
        w_attm = jnp.concatenate([wa[:, qkv_w:qkv_w + IDX_HEADS * IDX_HEAD_DIM], wa[:, :qkv_w]], axis=1).astype(BF16)
        n_misc = IDX_HEAD_DIM + IDX_HEADS
        w_attx = jnp.concatenate([wa[:, ATT_COLS - n_misc:], jnp.zeros((d, LANES - n_misc), w.dtype)], axis=1).astype(BF16)
        w_gate = w[:, c2:].astype(BF16)

        p_rwkv = _matmul(xb, w_rwkv, F32, bm=1024, bn=1152, name="proj_rwkv")
        p_sg = _matmul(xb, w_sg, F32, bm=1024, bn=1024, name="proj_sg")
        p_attm = _matmul(xb, w_attm, BF16, bm=1024, bn=1280, name="proj_att")
        p_attx = _matmul(xb, w_attx, F32, bm=1024, bn=LANES, name="proj_attx")
        p_gate = _matmul(xb, w_gate, F32, bm=1024, bn=1024, name="proj_gate")

        mu_pad = jnp.concatenate([rwkv_mu[l], jnp.zeros((RWKV_COLS_PAD - RWKV_COLS,), F32)]).reshape(1, -1)
        w2p = _pad_rows(rwkv_w2[l], LANES).astype(BF16)
        a2p = jnp.concatenate([jnp.zeros((DECAY_LORA, RWKV_DIM), F32), rwkv_a2[l]], axis=0).astype(BF16)
        g2p = _pad_rows(rwkv_g2[l], 2 * LANES).astype(BF16)
        y_rwkv = _rwkv(p_rwkv, mu_pad, row2(rwkv_w0[l]), row2(rwkv_a0[l]), row2(rwkv_k_k[l]), row2(rwkv_k_a[l]),
                       row2(rwkv_r_k[l]), row2(rwkv_gn_g[l]), row2(rwkv_gn_b[l]), w2p, a2p, g2p,
                       batch=batch, seq=seq)

        sg_bias = jnp.repeat(sg_b[l].T, SG_GROUP_DIM, axis=1)
        y_sg = _sgu(p_sg, row2(sg_ln_g[l]), row2(sg_ln_b[l]), sg_w[l], sg_bias)

        kidx2 = _kidx_ln(p_attx, row2(jnp.tile(idx_ln_g[l], 2)), row2(jnp.tile(idx_ln_b[l], 2)))
        w_t = p_attx[:, IDX_HEAD_DIM:IDX_HEAD_DIM + IDX_HEADS].T
        v_t = p_attm[:, ATT_MAIN_COLS - ATT_DIM:].reshape(batch, nb, Q_BLOCK, ATT_DIM).transpose(0, 1, 3, 2)
        y_att = _dsa(p_attm, v_t, kidx2, w_t, bias3, batch=batch, seq=seq)

        xf = _merge(y_rwkv, y_sg, y_att, p_gate, row2(b_gate[l]), w_branch[l].astype(BF16), w_o[l].astype(BF16),
                    xf, row2(ln2_g[l]), row2(ln2_b[l]))

        xf, _ = _ffn_ln(xf, ffn2_w_in[l].astype(BF16), ffn2_w_out[l].astype(BF16), row2(ln3_g[l]), row2(ln3_b[l]))

    return xf.reshape(batch, seq, d)
```

```python
import functools
import math

import jax
import jax.numpy as jnp
from jax import lax
from jax.experimental import pallas as pl
from jax.experimental.pallas import tpu as pltpu

F32 = jnp.float32
BF16 = jnp.bfloat16
I32 = jnp.int32

V7X_VMEM_BYTES = 64 * 1024 * 1024
VMEM_LIMIT_BYTES = V7X_VMEM_BYTES - 8 * 1024 * 1024
LANES = 128

D_MODEL = 2048
DEPTH = 4
RWKV_HEADS, RWKV_HEAD_DIM = 16, 64
RWKV_DIM = RWKV_HEADS * RWKV_HEAD_DIM
DECAY_LORA, AAA_LORA, GATE_LORA = 64, 64, 160
RWKV_GN_EPS = 64e-5
SG_GROUPS, SG_GROUP_DIM, SG_CHUNK = 4, 128, 128
SG_DIM = SG_GROUPS * SG_GROUP_DIM
ATT_HEADS, ATT_HEAD_DIM = 8, 64
ATT_DIM = ATT_HEADS * ATT_HEAD_DIM
IDX_HEADS, IDX_HEAD_DIM = 16, 64
TOPK_MAX = 256
Q_BLOCK = 128
REL_BUCKETS, REL_MAX_DIST = 32, 128
D_FF = 5632
LN_EPS = 1e-5
DEEPNORM_ALPHA = (2 * DEPTH) ** 0.25

RWKV_COLS = 3 * RWKV_DIM + DECAY_LORA + AAA_LORA + GATE_LORA
RWKV_LORA_PAD = 384
RWKV_COLS_PAD = 3 * RWKV_DIM + RWKV_LORA_PAD
SG_COLS = 2 * SG_DIM
ATT_MAIN_COLS = IDX_HEADS * IDX_HEAD_DIM + 3 * ATT_DIM
ATT_COLS = 3 * ATT_DIM + IDX_HEADS * IDX_HEAD_DIM + IDX_HEAD_DIM + IDX_HEADS
GATE_COLS = 3 * D_MODEL

RWKV_CHUNK = 64
DSA_SUPER = 4
NEG_BIG = -1e30


def _cparams(*sem):
    return pltpu.CompilerParams(dimension_semantics=sem, vmem_limit_bytes=VMEM_LIMIT_BYTES)


def _layer_norm(y, g, b, eps):
    mu = jnp.mean(y, axis=-1, keepdims=True)
    d = y - mu
    var = jnp.mean(d * d, axis=-1, keepdims=True)
    return d * lax.rsqrt(var + eps) * g + b


def _dot(a, b):
    return jnp.dot(a, b, preferred_element_type=F32)


def _dot_nt(a, b):
    return lax.dot_general(a, b, (((1,), (1,)), ((), ())), preferred_element_type=F32)


def _ffn_kernel(x_ref, wg_ref, wu_ref, wo_ref, g_ref, b_ref, o_ref, ob_ref, xb_ref, acc_ref):
    j = pl.program_id(1)

    @pl.when(j == 0)
    def _():
        xb_ref[...] = x_ref[...].astype(BF16)
        acc_ref[...] = jnp.zeros_like(acc_ref)

    xb = xb_ref[...]
    gate = _dot(xb, wg_ref[...])
    up = _dot(xb, wu_ref[...])
    h = (gate * jax.nn.sigmoid(gate) * up).astype(BF16)
    acc_ref[...] += _dot(h, wo_ref[...])

    @pl.when(j == pl.num_programs(1) - 1)
    def _():
        y = DEEPNORM_ALPHA * x_ref[...] + 0.5 * acc_ref[...]
        out = _layer_norm(y, g_ref[...], b_ref[...], LN_EPS)
        o_ref[...] = out
        ob_ref[...] = out.astype(BF16)


def _ffn_ln(x, w_in, w_out, g, b, *, tm=512, tf=512):
    t, d = x.shape
    f = w_out.shape[0]
    nj = f // tf
    return pl.pallas_call(
        _ffn_kernel,
        grid=(t // tm, nj),
        in_specs=[
            pl.BlockSpec((tm, d), lambda i, j: (i, 0)),
            pl.BlockSpec((d, tf), lambda i, j: (0, j)),
            pl.BlockSpec((d, tf), lambda i, j: (0, j + nj)),
            pl.BlockSpec((tf, d), lambda i, j: (j, 0)),
            pl.BlockSpec((1, d), lambda i, j: (0, 0)),
            pl.BlockSpec((1, d), lambda i, j: (0, 0)),
        ],
        out_specs=[
            pl.BlockSpec((tm, d), lambda i, j: (i, 0)),
            pl.BlockSpec((tm, d), lambda i, j: (i, 0)),
        ],
        out_shape=[jax.ShapeDtypeStruct((t, d), F32), jax.ShapeDtypeStruct((t, d), BF16)],
        scratch_shapes=[pltpu.VMEM((tm, d), BF16), pltpu.VMEM((tm, d), F32)],
        compiler_params=_cparams("parallel", "arbitrary"),
        name="ffn_ln",
    )(x, w_in, w_in, w_out, g, b)


def _mm_kernel(x_ref, w_ref, o_ref):
    o_ref[...] = _dot(x_ref[...], w_ref[...]).astype(o_ref.dtype)


def _matmul(x, w, out_dtype, *, bm, bn, name):
    t, k = x.shape
    n = w.shape[1]
    return pl.pallas_call(
        _mm_kernel,
        grid=(t // bm, n // bn),
        in_specs=[pl.BlockSpec((bm, k), lambda i, j: (i, 0)), pl.BlockSpec((k, bn), lambda i, j: (0, j))],
        out_specs=pl.BlockSpec((bm, bn), lambda i, j: (i, j)),
        out_shape=jax.ShapeDtypeStruct((t, n), out_dtype),
        compiler_params=_cparams("parallel", "arbitrary"),
        name=name,
    )(x, w)


def _split3(x):
    h = x.astype(BF16)
    r1 = x - h.astype(F32)
    m = r1.astype(BF16)
    lo = (r1 - m.astype(F32)).astype(BF16)
    return h, m, lo


def _dot3(x, w_bf16):
    h, m, lo = _split3(x)
    return _dot(h, w_bf16) + _dot(m, w_bf16) + _dot(lo, w_bf16)


def _rwkv_kernel(r_ref, k_ref, v_ref, lo_ref, mur_ref, muk_ref, muv_ref, mulo_ref,
                 w0_ref, a0_ref, kk_ref, ka_ref, rk_ref, gg_ref, gb_ref,
                 w2_ref, a2_ref, g2_ref, o_ref,
                 s_ref, pr_ref, pk_ref, pv_ref, plo_ref):
    tb = pl.program_id(2)
    ts = r_ref.shape[0]
    C = RWKV_CHUNK
    H2 = 2 * C
    nch = ts // C

    @pl.when(tb == 0)
    def _():
        s_ref[...] = jnp.zeros_like(s_ref)
        pr_ref[...] = jnp.zeros_like(pr_ref)
        pk_ref[...] = jnp.zeros_like(pk_ref)
        pv_ref[...] = jnp.zeros_like(pv_ref)
        plo_ref[...] = jnp.zeros_like(plo_ref)

    def shift(x_ref, prev_ref, mu_ref):
        x = x_ref[...]
        first = lax.broadcasted_iota(I32, x.shape, 0) == 0
        xp = jnp.where(first, prev_ref[...], pltpu.roll(x, 1, 0))
        prev_ref[...] = x[ts - 1:ts, :]
        return x + (xp - x) * mu_ref[...]

    r = shift(r_ref, pr_ref, mur_ref)
    k = shift(k_ref, pk_ref, muk_ref)
    v = shift(v_ref, pv_ref, muv_ref)
    lo = shift(lo_ref, plo_ref, mulo_ref)

    lane = lax.broadcasted_iota(I32, (LANES, LANES), 1)
    row = lax.broadcasted_iota(I32, (LANES, LANES), 0)
    same_head = (lane < 64) == (row < 64)
    seg_ones = jnp.where(same_head, 1.0, 0.0).astype(BF16)

    wa = lo[:, :LANES]
    zw = w0_ref[...] + _dot(jnp.tanh(wa).astype(BF16), w2_ref[...])
    nz = -zw
    softplus = jnp.maximum(nz, 0.0) + jnp.log1p(jnp.exp(-jnp.abs(nz)))
    logw = -jnp.exp(-softplus - 0.5)
    a = jax.nn.sigmoid(a0_ref[...] + _dot(wa.astype(BF16), a2_ref[...]))
    g = _dot(jax.nn.sigmoid(lo[:, LANES:]).astype(BF16), g2_ref[...])

    kk = k * kk_ref[...]
    n2 = _dot3(kk * kk, seg_ones)
    kkn = kk / jnp.maximum(jnp.sqrt(n2), 1e-12)
    k2 = k * (1.0 + (a - 1.0) * ka_ref[...])

    rin = lax.broadcasted_iota(I32, (ts, LANES), 0) & (C - 1)
    cum = logw
    step = 1
    while step < C:
        cum = cum + jnp.where(rin >= step, pltpu.roll(cum, step, 0), 0.0)
        step *= 2

    aa = -kkn
    bb = kkn * a
    p_inv = jnp.exp(-cum)
    at = aa * jnp.exp(cum - logw)
    rt = r * jnp.exp(cum)
    kt = k2 * p_inv
    bt = bb * p_inv

    head_a = lax.broadcasted_iota(I32, (ts, LANES), 1) < 64
    head_a_c = lax.broadcasted_iota(I32, (C, LANES), 1) < 64

    def by_head(x):
        xb = x.astype(BF16)
        z = jnp.zeros_like(xb)
        return jnp.where(head_a, xb, z), jnp.where(head_a, z, xb)

    def rows(x, c):
        return x[c * C:(c + 1) * C, :]

    def stack(pair, c):
        return jnp.concatenate([rows(pair[0], c), rows(pair[1], c)], axis=0)

    def stack_val(x):
        xb = x.astype(BF16)
        z = jnp.zeros_like(xb)
        return jnp.concatenate([jnp.where(head_a_c, xb, z), jnp.where(head_a_c, z, xb)], axis=0)

    def bdiag(x):
        xb = x.astype(BF16)
        return jnp.where(same_head, jnp.concatenate([xb, xb], axis=0), jnp.zeros((H2, H2), BF16))

    crow = lax.broadcasted_iota(I32, (C, H2), 0)
    ccol = lax.broadcasted_iota(I32, (C, H2), 1)
    ccol = jnp.where(ccol >= C, ccol - C, ccol)
    strict = ccol < crow
    incl = ccol <= crow
    eye_rp = jnp.where(ccol == crow, 1.0, 0.0)
    eye128 = lane == row

    bt_h = by_head(bt)
    kt_h = by_head(kt)
    at_h = by_head(at)
    v_h = by_head(v)
    at_b = at.astype(BF16)
    rt_b = rt.astype(BF16)
    ch = range(nch)

    ar = [jnp.concatenate([rows(at_b, c), rows(rt_b, c)], axis=0) for c in ch]
    xb = [_dot_nt(ar[c], stack(bt_h, c)) for c in ch]
    xk = [_dot_nt(ar[c], stack(kt_h, c)) for c in ch]
    a_ab = [jnp.where(strict, xb[c][:C], 0.0) for c in ch]
    a_rb = [jnp.where(incl, xb[c][C:], 0.0).astype(BF16) for c in ch]
    a_ak = [jnp.where(strict, xk[c][:C], 0.0).astype(BF16) for c in ch]
    a_rk = [jnp.where(incl, xk[c][C:], 0.0).astype(BF16) for c in ch]

    tinv = [eye_rp + a_ab[c] for c in ch]
    apow = a_ab
    n = 1
    while n < C // 2:
        apow = [_dot(apow[c].astype(BF16), bdiag(apow[c])) for c in ch]
        tinv = [tinv[c] + _dot(tinv[c].astype(BF16), bdiag(apow[c])) for c in ch]
        n *= 2
    tinv = [tinv[c].astype(BF16) for c in ch]

    gv = [_dot(a_ak[c], stack(v_h, c)) for c in ch]
    w1 = [_dot(tinv[c], stack(at_h, c)) for c in ch]
    w2 = [_dot(tinv[c], stack_val(gv[c])) for c in ch]
    rq = [(rows(rt, c) + _dot(a_rb[c], stack_val(w1[c]))).astype(BF16) for c in ch]
    y0 = [_dot(a_rk[c], stack(v_h, c)) + _dot(a_rb[c], stack_val(w2[c])) for c in ch]

    cl = [cum[c * C + C - 1:(c + 1) * C, :] for c in ch]
    p_end = [jnp.exp(cl[c] - rows(cum, c)) for c in ch]
    kbt = [jnp.concatenate([rows(k2, c) * p_end[c], rows(bb, c) * p_end[c]], axis=0).T.astype(BF16) for c in ch]
    v_b = v.astype(BF16)
    n_add = [jnp.where(same_head, _dot(kbt[c], jnp.concatenate([rows(v_b, c), w2[c].astype(BF16)], axis=0)), 0.0)
             for c in ch]
    zeros_c = jnp.zeros((C, LANES), BF16)
    m_mat = [(jnp.where(same_head, _dot(kbt[c], jnp.concatenate([zeros_c, w1[c].astype(BF16)], axis=0)), 0.0)
              + jnp.where(eye128, jnp.exp(cl[c]), 0.0)).astype(BF16) for c in ch]

    s_cur = s_ref[...]
    ys = []
    for c in ch:
        s_b = s_cur.astype(BF16)
        ys.append(_dot(rq[c], s_b) + y0[c])
        s_cur = _dot(m_mat[c], s_b) + n_add[c]
    s_ref[...] = s_cur
    y = jnp.concatenate(ys, axis=0)

    inv_n = 1.0 / RWKV_HEAD_DIM
    mu = _dot3(y, seg_ones) * inv_n
    d = y - mu
    var = _dot3(d * d, seg_ones) * inv_n
    yn = d * lax.rsqrt(var + RWKV_GN_EPS) * gg_ref[...] + gb_ref[...]
    bonus = _dot3(r * k2 * rk_ref[...], seg_ones) * v
    o_ref[...] = ((yn + bonus) * g).astype(o_ref.dtype)


def _rwkv(p, mu_pad, w0, a0, k_k, k_a, r_k, gn_g, gn_b, w2p, a2p, g2p, *, batch, seq, ts=1024):
    t = p.shape[0]
    npair = RWKV_HEADS // 2
    nt = seq // ts
    nlora = 3 * RWKV_DIM // RWKV_LORA_PAD

    def col(off):
        return pl.BlockSpec((ts, LANES), lambda b, h, s, off=off: (b * nt + s, off + h))

    def prow(off=0):
        return pl.BlockSpec((1, LANES), lambda b, h, s, off=off: (0, off + h))

    f32 = lambda *shape: pltpu.VMEM(shape, F32)
    return pl.pallas_call(
        _rwkv_kernel,
        grid=(batch, npair, nt),
        in_specs=[
            col(0), col(npair), col(2 * npair),
            pl.BlockSpec((ts, RWKV_LORA_PAD), lambda b, h, s: (b * nt + s, nlora)),
            prow(0), prow(npair), prow(2 * npair),
            pl.BlockSpec((1, RWKV_LORA_PAD), lambda b, h, s: (0, nlora)),
            prow(), prow(), prow(), prow(), prow(), prow(), prow(),
            pl.BlockSpec((LANES, LANES), lambda b, h, s: (0, h)),
            pl.BlockSpec((LANES, LANES), lambda b, h, s: (0, h)),
            pl.BlockSpec((2 * LANES, LANES), lambda b, h, s: (0, h)),
        ],
        out_specs=pl.BlockSpec((ts, LANES), lambda b, h, s: (b * nt + s, h)),
        out_shape=jax.ShapeDtypeStruct((t, RWKV_DIM), BF16),
        scratch_shapes=[
            f32(LANES, LANES), f32(1, LANES), f32(1, LANES), f32(1, LANES), f32(1, RWKV_LORA_PAD),
        ],
        compiler_params=_cparams("parallel", "parallel", "arbitrary"),
        name="rwkv7",
    )(p, p, p, p, mu_pad, mu_pad, mu_pad, mu_pad, w0, a0, k_k, k_a, r_k, gn_g, gn_b, w2p, a2p, g2p)


def _sgu_kernel(p_ref, lg_ref, lb_ref, w_ref, bs_ref, o_ref):
    tm = p_ref.shape[0]
    p = p_ref[...]
    z = 0.5 * p * (1.0 + lax.erf(p * math.sqrt(0.5)))
    u = z[:, :SG_DIM]
    v = _layer_norm(z[:, SG_DIM:], lg_ref[...], lb_ref[...], LN_EPS).astype(BF16)
    rowi = lax.broadcasted_iota(I32, (SG_CHUNK, SG_CHUNK), 0)
    colj = lax.broadcasted_iota(I32, (SG_CHUNK, SG_CHUNK), 1)
    causal = colj <= rowi
    for g in range(SG_GROUPS):
        wg = jnp.where(causal, w_ref[g], 0.0).astype(BF16)
        gs = slice(g * SG_GROUP_DIM, (g + 1) * SG_GROUP_DIM)
        for c in range(tm // SG_CHUNK):
            rs = slice(c * SG_CHUNK, (c + 1) * SG_CHUNK)
            mixed = _dot(wg, v[rs, gs]) + bs_ref[:, gs]
            o_ref[rs, gs] = (u[rs, gs] * mixed).astype(o_ref.dtype)


def _sgu(p, ln_g, ln_b, w_s, bias_full, *, tm=512):
    t = p.shape[0]
    return pl.pallas_call(
        _sgu_kernel,
        grid=(t // tm,),
        in_specs=[
            pl.BlockSpec((tm, SG_COLS), lambda i: (i, 0)),
            pl.BlockSpec((1, SG_DIM), lambda i: (0, 0)),
            pl.BlockSpec((1, SG_DIM), lambda i: (0, 0)),
            pl.BlockSpec((SG_GROUPS, SG_CHUNK, SG_CHUNK), lambda i: (0, 0, 0)),
            pl.BlockSpec((SG_CHUNK, SG_DIM), lambda i: (0, 0)),
        ],
        out_specs=pl.BlockSpec((tm, SG_DIM), lambda i: (i, 0)),
        out_shape=jax.ShapeDtypeStruct((t, SG_DIM), BF16),
        compiler_params=_cparams("parallel"),
        name="sgu",
    )(p, ln_g, ln_b, w_s, bias_full)


def _kidx_kernel(x_ref, g_ref, b_ref, o_ref):
    x = x_ref[...]
    lane = lax.broadcasted_iota(I32, x.shape, 1)
    x2 = jnp.where(lane < IDX_HEAD_DIM, x, pltpu.roll(x, IDX_HEAD_DIM, 1))
    o_ref[...] = _layer_norm(x2, g_ref[...], b_ref[...], LN_EPS).astype(o_ref.dtype)


def _kidx_ln(x, g2, b2, *, tm=2048):
    t = x.shape[0]
    return pl.pallas_call(
        _kidx_kernel,
        grid=(t // tm,),
        in_specs=[pl.BlockSpec((tm, LANES), lambda i: (i, 0)),
                  pl.BlockSpec((1, LANES), lambda i: (0, 0)),
                  pl.BlockSpec((1, LANES), lambda i: (0, 0))],
        out_specs=pl.BlockSpec((tm, LANES), lambda i: (i, 0)),
        out_shape=jax.ShapeDtypeStruct((t, LANES), BF16),
        compiler_params=_cparams("parallel"),
        name="kidx_ln",
    )(x, g2, b2)


def _dsa_kernel(qi_ref, q_ref, k_ref, vt_ref, kx_ref, wt_ref, b3_ref, o_ref,
                rt_ref, qt_ref, key_ref, m_ref, l_ref, acc_ref, *, topk):
    i = pl.program_id(1)
    QB = Q_BLOCK
    head_a = lax.broadcasted_iota(I32, (QB, LANES), 1) < 64

    def pair_rhs(x):
        zero = jnp.zeros_like(x)
        return jnp.concatenate([jnp.where(head_a, x, zero), jnp.where(head_a, zero, x)], axis=0)

    n_ip = IDX_HEADS // 2
    n_ap = ATT_HEADS // 2
    for hp in range(n_ip):
        rt_ref[hp] = pair_rhs(qi_ref[:, hp * LANES:(hp + 1) * LANES])
    for hp in range(n_ap):
        qt_ref[hp] = pair_rhs(q_ref[:, hp * LANES:(hp + 1) * LANES])

    scale = (IDX_HEAD_DIM ** -0.5) * (IDX_HEADS ** -0.5)
    wt = wt_ref[...] * scale
    wrows = [jnp.concatenate([wt[2 * hp:2 * hp + 1, :], wt[2 * hp + 1:2 * hp + 2, :]], axis=1)
             for hp in range(n_ip)]

    int_min = jnp.int32(-2 ** 31)

    SK = DSA_SUPER * QB
    n_super = (i + DSA_SUPER) // DSA_SUPER
    krow = lax.broadcasted_iota(I32, (SK, LANES), 0)
    qlane = lax.broadcasted_iota(I32, (SK, LANES), 1)

    def score_step(sc_i, carry):
        ks = pl.ds(pl.multiple_of(sc_i * SK, SK), SK)
        kc = kx_ref[ks, :]
        acc = jnp.maximum(_dot_nt(kc, rt_ref[0]), 0.0) * wrows[0]
        for hp in range(1, n_ip):
            acc = acc + jnp.maximum(_dot_nt(kc, rt_ref[hp]), 0.0) * wrows[hp]
        sc = acc[:, :QB] + acc[:, QB:]
        bits = lax.bitcast_convert_type(sc, I32)
        key = bits ^ ((bits >> 31) & jnp.int32(0x7FFFFFFF))
        valid = (sc_i * SK + krow) <= (i * QB + qlane)
        key_ref[ks, :] = jnp.where(valid, key, int_min)
        return carry

    lax.fori_loop(0, n_super, score_step, 0)

    def bit_step(bi, t):
        cand = t + lax.shift_left(jnp.int32(1), 31 - bi)

        def count_step(sc_i, cnt):
            ks = pl.ds(pl.multiple_of(sc_i * SK, SK), SK)
            ge = jnp.where(key_ref[ks, :] >= cand, 1, 0)
            parts = [ge[r8 * 8:(r8 + 1) * 8, :] for r8 in range(SK // 8)]
            while len(parts) > 1:
                parts = [parts[j] + parts[j + 1] for j in range(0, len(parts), 2)]
            return cnt + parts[0]

        cnt8 = lax.fori_loop(0, n_super, count_step, jnp.zeros((8, LANES), I32))
        cnt = jnp.sum(cnt8, axis=0, keepdims=True)
        return jnp.where(cnt >= topk, cand, t)

    thr = lax.fori_loop(0, 32, bit_step, jnp.full((1, LANES), int_min, I32))

    thr = jnp.maximum(thr, int_min + 1)
    m_ref[...] = jnp.full(m_ref.shape, NEG_BIG, F32)
    l_ref[...] = jnp.zeros(l_ref.shape, F32)
    acc_ref[...] = jnp.zeros(acc_ref.shape, F32)
    att_scale = ATT_HEAD_DIM ** -0.5

    def attend_step(sc_i, carry):
        base = sc_i * DSA_SUPER
        ks = pl.ds(pl.multiple_of(sc_i * SK, SK), SK)
        madd = jnp.where(key_ref[ks, :] >= thr, 0.0, 2.0 * NEG_BIG)
        madd2 = jnp.concatenate([madd, madd], axis=1)
        raw = [_dot_nt(k_ref[ks, hp * LANES:(hp + 1) * LANES], qt_ref[hp]) for hp in range(n_ap)]
        kinds = [jnp.clip(i - (base + j), 0, 2) for j in range(DSA_SUPER)]
        ps = []
        for hp in range(n_ap):
            bias = jnp.concatenate([b3_ref[kinds[j], hp] for j in range(DSA_SUPER)], axis=0)
            s = raw[hp] * att_scale + bias + madd2
            m_old = m_ref[hp]
            m_new = jnp.maximum(m_old, jnp.max(s, axis=0, keepdims=True))
            p = jnp.exp(s - m_new)
            alpha = jnp.exp(m_old - m_new)
            l_ref[hp] = alpha * l_ref[hp] + jnp.sum(p, axis=0, keepdims=True)
            m_ref[hp] = m_new
            ps.append((p.astype(BF16), alpha))
        for hp in range(n_ap):
            vt = jnp.concatenate([vt_ref[0, base + j, hp * LANES:(hp + 1) * LANES, :] for j in range(DSA_SUPER)],
                                 axis=1)
            acc_ref[hp] = acc_ref[hp] * ps[hp][1] + _dot(vt, ps[hp][0])
        return carry

    lax.fori_loop(0, n_super, attend_step, 0)

    for hp in range(n_ap):
        o2 = acc_ref[hp] * (1.0 / l_ref[hp])
        pair_t = jnp.concatenate([o2[:64, :QB], o2[64:, QB:]], axis=0)
        o_ref[:, hp * LANES:(hp + 1) * LANES] = pair_t.T.astype(o_ref.dtype)


def _dsa(p_main, vt, kidx2, w_t, bias3, *, batch, seq):
    t = p_main.shape[0]
    nb = seq // Q_BLOCK
    assert nb % DSA_SUPER == 0
    top_k = min(TOPK_MAX, seq // 4)
    n_ip, n_ap = IDX_HEADS // 2, ATT_HEADS // 2
    qi_w = IDX_HEADS * IDX_HEAD_DIM
    return pl.pallas_call(
        functools.partial(_dsa_kernel, topk=top_k),
        grid=(batch, nb),
        in_specs=[
            pl.BlockSpec((Q_BLOCK, qi_w), lambda b, i: (b * nb + i, 0)),
            pl.BlockSpec((Q_BLOCK, ATT_DIM), lambda b, i: (b * nb + i, qi_w // ATT_DIM)),
            pl.BlockSpec((seq, ATT_DIM), lambda b, i: (b, qi_w // ATT_DIM + 1)),
            pl.BlockSpec((1, nb, ATT_DIM, Q_BLOCK), lambda b, i: (b, 0, 0, 0)),
            pl.BlockSpec((seq, LANES), lambda b, i: (b, 0)),
            pl.BlockSpec((IDX_HEADS, Q_BLOCK), lambda b, i: (0, b * nb + i)),
            pl.BlockSpec((3, n_ap, Q_BLOCK, 2 * Q_BLOCK), lambda b, i: (0, 0, 0, 0)),
        ],
        out_specs=pl.BlockSpec((Q_BLOCK, ATT_DIM), lambda b, i: (b * nb + i, 0)),
        out_shape=jax.ShapeDtypeStruct((t, ATT_DIM), BF16),
        scratch_shapes=[
            pltpu.VMEM((n_ip, 2 * Q_BLOCK, LANES), BF16),
            pltpu.VMEM((n_ap, 2 * Q_BLOCK, LANES), BF16),
            pltpu.VMEM((seq, Q_BLOCK), I32),
            pltpu.VMEM((n_ap, 1, 2 * Q_BLOCK), F32),
            pltpu.VMEM((n_ap, 1, 2 * Q_BLOCK), F32),
            pltpu.VMEM((n_ap, LANES, 2 * Q_BLOCK), F32),
        ],
        compiler_params=_cparams("parallel", "arbitrary"),
        name="dsa",
    )(p_main, p_main, p_main, vt, kidx2, w_t, bias3)


def _merge_kernel(yr_ref, ys_ref, ya_ref, pg_ref, bg_ref, wb_ref, wo_ref, x_ref, g_ref, b_ref, o_ref):
    d = D_MODEL
    z = _dot(yr_ref[...], wb_ref[0:RWKV_DIM, :])
    merged = jax.nn.sigmoid(pg_ref[:, 0:d] + bg_ref[:, 0:d]) * z
    z = _dot(ys_ref[...], wb_ref[RWKV_DIM:RWKV_DIM + SG_DIM, :])
    merged = merged + jax.nn.sigmoid(pg_ref[:, d:2 * d] + bg_ref[:, d:2 * d]) * z
    z = _dot(ya_ref[...], wb_ref[RWKV_DIM + SG_DIM:, :])
    merged = merged + jax.nn.sigmoid(pg_ref[:, 2 * d:] + bg_ref[:, 2 * d:]) * z
    mix = _dot(merged.astype(BF16), wo_ref[...])
    y = DEEPNORM_ALPHA * x_ref[...] + mix
    o_ref[...] = _layer_norm(y, g_ref[...], b_ref[...], LN_EPS)


def _merge(y_rwkv, y_sg, y_att, p_gate, b_gate, w_branch, w_o, x, g, b, *, tm=256):
    t, d = x.shape
    rowblk = lambda w: pl.BlockSpec((tm, w), lambda i: (i, 0))
    const = lambda r, c: pl.BlockSpec((r, c), lambda i: (0, 0))
    return pl.pallas_call(
        _merge_kernel,
        grid=(t // tm,),
        in_specs=[rowblk(RWKV_DIM), rowblk(SG_DIM), rowblk(ATT_DIM), rowblk(GATE_COLS), const(1, GATE_COLS),
                  const(d, d), const(d, d), rowblk(d), const(1, d), const(1, d)],
        out_specs=rowblk(d),
        out_shape=jax.ShapeDtypeStruct((t, d), F32),
        compiler_params=_cparams("parallel"),
        name="merge",
    )(y_rwkv, y_sg, y_att, p_gate, b_gate, w_branch, w_o, x, g, b)


def _rel_bucket(dist):
    max_exact = REL_BUCKETS // 2
    d_f = jnp.maximum(dist, 1).astype(F32)
    large = max_exact + (jnp.log(d_f / max_exact) / math.log(REL_MAX_DIST / max_exact)
                         * (REL_BUCKETS - max_exact)).astype(I32)
    large = jnp.minimum(large, REL_BUCKETS - 1)
    return jnp.where(dist < max_exact, dist, large)


def _bias_tables(rel_bias):
    n_ap = ATT_HEADS // 2
    rk = jnp.arange(Q_BLOCK, dtype=I32)[:, None]
    lq = jnp.arange(2 * Q_BLOCK, dtype=I32)[None, :] % Q_BLOCK
    per_pair = [jnp.concatenate([jnp.broadcast_to(rel_bias[:, 2 * hp, None], (REL_BUCKETS, Q_BLOCK)),
                                 jnp.broadcast_to(rel_bias[:, 2 * hp + 1, None], (REL_BUCKETS, Q_BLOCK))], axis=1)
                for hp in range(n_ap)]
    tiles = []
    for delta in (0, 1, 2):
        bucket = _rel_bucket(jnp.maximum(delta * Q_BLOCK + lq - rk, 0))
        pairs = []
        for hp in range(n_ap):
            t = jnp.zeros((Q_BLOCK, 2 * Q_BLOCK), F32)
            for bkt in range(REL_BUCKETS):
                t = t + jnp.where(bucket == bkt, per_pair[hp][bkt][None, :], 0.0)
            pairs.append(t)
        tiles.append(jnp.stack(pairs))
    return jnp.stack(tiles)


def _pad_rows(w, rows):
    return jnp.concatenate([w, jnp.zeros((rows - w.shape[0], w.shape[1]), w.dtype)], axis=0)


def kernel(x, ffn1_w_in, ffn1_w_out, ln1_g, ln1_b, w_in, b_gate, rwkv_mu, rwkv_w0, rwkv_w2, rwkv_a0, rwkv_a2, rwkv_g2, rwkv_k_k, rwkv_k_a, rwkv_r_k, rwkv_gn_g, rwkv_gn_b, sg_ln_g, sg_ln_b, sg_w, sg_b, idx_ln_g, idx_ln_b, rel_bias, w_branch, w_o, ln2_g, ln2_b, ffn2_w_in, ffn2_w_out, ln3_g, ln3_b):
    batch, seq, d = x.shape
    t = batch * seq
    nb = seq // Q_BLOCK
    depth = w_in.shape[0]
    xf = x.reshape(t, d)
    bias3 = _bias_tables(rel_bias)
    row2 = lambda v: v.reshape(1, -1)

    for l in range(depth):
        xf, xb = _ffn_ln(xf, ffn1_w_in[l].astype(BF16), ffn1_w_out[l].astype(BF16), row2(ln1_g[l]), row2(ln1_b[l]))

        w = w_in[l]
        c0 = RWKV_COLS
        c1 = c0 + SG_COLS
        c2 = c1 + ATT_COLS
        w_rwkv = jnp.concatenate([w[:, :c0], jnp.zeros((d, RWKV_COLS_PAD - RWKV_COLS), w.dtype)], axis=1).astype(BF16)
        w_sg = w[:, c0:c1].astype(BF16)
        wa = w[:, c1:c2]
        qkv_w = 3 * ATT_DIM
        w_attm = jnp.concatenate([wa[:, qkv_w:qkv_w + IDX_HEADS * IDX_HEAD_DIM], wa[:, :qkv_w]], axis=1).astype(BF16)
        n_misc = IDX_HEAD_DIM + IDX_HEADS
        w_attx = jnp.concatenate([wa[:, ATT_COLS - n_misc:], jnp.zeros((d, LANES - n_misc), w.dtype)], axis=1).astype(BF16)
        w_gate = w[:, c2:].astype(BF16)

        p_rwkv = _matmul(xb, w_rwkv, F32, bm=1024, bn=1152, name="proj_rwkv")
        p_sg = _matmul(xb, w_sg, F32, bm=1024, bn=1024, name="proj_sg")
        p_attm = _matmul(xb, w_attm, BF16, bm=1024, bn=1280, name="proj_att")
        p_attx = _matmul(xb, w_attx, F32, bm=1024, bn=LANES, name="proj_attx")
        p_gate = _matmul(xb, w_gate, F32, bm=1024, bn=1024, name="proj_gate")

        mu_pad = jnp.concatenate([rwkv_mu[l], jnp.zeros((RWKV_COLS_PAD - RWKV_COLS,), F32)]).reshape(1, -1)
        w2p = _pad_rows(rwkv_w2[l], LANES).astype(BF16)
        a2p = jnp.concatenate([jnp.zeros((DECAY_LORA, RWKV_DIM), F32), rwkv_a2[l]], axis=0).astype(BF16)
        g2p = _pad_rows(rwkv_g2[l], 2 * LANES).astype(BF16)
        y_rwkv = _rwkv(p_rwkv, mu_pad, row2(rwkv_w0[l]), row2(rwkv_a0[l]), row2(rwkv_k_k[l]), row2(rwkv_k_a[l]),
                       row2(rwkv_r_k[l]), row2(rwkv_gn_g[l]), row2(rwkv_gn_b[l]), w2p, a2p, g2p,
                       batch=batch, seq=seq)

        sg_bias = jnp.repeat(sg_b[l].T, SG_GROUP_DIM, axis=1)
        y_sg = _sgu(p_sg, row2(sg_ln_g[l]), row2(sg_ln_b[l]), sg_w[l], sg_bias)

        kidx2 = _kidx_ln(p_attx, row2(jnp.tile(idx_ln_g[l], 2)), row2(jnp.tile(idx_ln_b[l], 2)))
        w_t = p_attx[:, IDX_HEAD_DIM:IDX_HEAD_DIM + IDX_HEADS].T
        v_t = p_attm[:, ATT_MAIN_COLS - ATT_DIM:].reshape(batch, nb, Q_BLOCK, ATT_DIM).transpose(0, 1, 3, 2)
        y_att = _dsa(p_attm, v_t, kidx2, w_t, bias3, batch=batch, seq=seq)

        xf = _merge(y_rwkv, y_sg, y_att, p_gate, row2(b_gate[l]), w_branch[l].astype(BF16), w_o[l].astype(BF16),
                    xf, row2(ln2_g[l]), row2(ln2_b[l]))

        xf, _ = _ffn_ln(xf, ffn2_w_in[l].astype(BF16), ffn2_w_out[l].astype(BF16), row2(ln3_g[l]), row2(ln3_b[l]))

    return xf.reshape(batch, seq, d)
```

```python
import functools
import math

import jax
import jax.numpy as jnp
from jax import lax
from jax.experimental import pallas as pl
from jax.experimental.pallas import tpu as pltpu

F32 = jnp.float32
BF16 = jnp.bfloat16
I32 = jnp.int32

V7X_VMEM_BYTES = 64 * 1024 * 1024
VMEM_LIMIT_BYTES = V7X_VMEM_BYTES - 8 * 1024 * 1024
LANES = 128

D_MODEL = 2048
DEPTH = 4
RWKV_HEADS, RWKV_HEAD_DIM = 16, 64
RWKV_DIM = RWKV_HEADS * RWKV_HEAD_DIM
DECAY_LORA, AAA_LORA, GATE_LORA = 64, 64, 160
RWKV_GN_EPS = 64e-5
SG_GROUPS, SG_GROUP_DIM, SG_CHUNK = 4, 128, 128
SG_DIM = SG_GROUPS * SG_GROUP_DIM
ATT_HEADS, ATT_HEAD_DIM = 8, 64
ATT_DIM = ATT_HEADS * ATT_HEAD_DIM
IDX_HEADS, IDX_HEAD_DIM = 16, 64
TOPK_MAX = 256
Q_BLOCK = 128
REL_BUCKETS, REL_MAX_DIST = 32, 128
D_FF = 5632
LN_EPS = 1e-5
DEEPNORM_ALPHA = (2 * DEPTH) ** 0.25

RWKV_COLS = 3 * RWKV_DIM + DECAY_LORA + AAA_LORA + GATE_LORA
RWKV_LORA_PAD = 384
RWKV_COLS_PAD = 3 * RWKV_DIM + RWKV_LORA_PAD
SG_COLS = 2 * SG_DIM
ATT_MAIN_COLS = IDX_HEADS * IDX_HEAD_DIM + 3 * ATT_DIM
ATT_COLS = 3 * ATT_DIM + IDX_HEADS * IDX_HEAD_DIM + IDX_HEAD_DIM + IDX_HEADS
GATE_COLS = 3 * D_MODEL

RWKV_CHUNK = 64
DSA_SUPER = 4
DSA_BIT_GROUP = 4
NEG_BIG = -1e30


def _cparams(*sem):
    return pltpu.CompilerParams(dimension_semantics=sem, vmem_limit_bytes=VMEM_LIMIT_BYTES)


def _layer_norm(y, g, b, eps):
    mu = jnp.mean(y, axis=-1, keepdims=True)
    d = y - mu
    var = jnp.mean(d * d, axis=-1, keepdims=True)
    return d * lax.rsqrt(var + eps) * g + b


def _dot(a, b):
    return jnp.dot(a, b, preferred_element_type=F32)


def _dot_nt(a, b):
    return lax.dot_general(a, b, (((1,), (1,)), ((), ())), preferred_element_type=F32)


def _ffn_kernel(x_ref, wg_ref, wu_ref, wo_ref, g_ref, b_ref, o_ref, ob_ref, xb_ref, acc_ref):
    j = pl.program_id(1)

    @pl.when(j == 0)
    def _():
        xb_ref[...] = x_ref[...].astype(BF16)
        acc_ref[...] = jnp.zeros_like(acc_ref)

    xb = xb_ref[...]
    gate = _dot(xb, wg_ref[...])
    up = _dot(xb, wu_ref[...])
    h = (gate * jax.nn.sigmoid(gate) * up).astype(BF16)
    acc_ref[...] += _dot(h, wo_ref[...])

    @pl.when(j == pl.num_programs(1) - 1)
    def _():
        y = DEEPNORM_ALPHA * x_ref[...] + 0.5 * acc_ref[...]
        out = _layer_norm(y, g_ref[...], b_ref[...], LN_EPS)
        o_ref[...] = out
        ob_ref[...] = out.astype(BF16)


def _ffn_ln(x, w_in, w_out, g, b, *, tm=512, tf=512):
    t, d = x.shape
    f = w_out.shape[0]
    nj = f // tf
    return pl.pallas_call(
        _ffn_kernel,
        grid=(t // tm, nj),
        in_specs=[
            pl.BlockSpec((tm, d), lambda i, j: (i, 0)),
            pl.BlockSpec((d, tf), lambda i, j: (0, j)),
            pl.BlockSpec((d, tf), lambda i, j: (0, j + nj)),
            pl.BlockSpec((tf, d), lambda i, j: (j, 0)),
            pl.BlockSpec((1, d), lambda i, j: (0, 0)),
            pl.BlockSpec((1, d), lambda i, j: (0, 0)),
        ],
        out_specs=[
            pl.BlockSpec((tm, d), lambda i, j: (i, 0)),
            pl.BlockSpec((tm, d), lambda i, j: (i, 0)),
        ],
        out_shape=[jax.ShapeDtypeStruct((t, d), F32), jax.ShapeDtypeStruct((t, d), BF16)],
        scratch_shapes=[pltpu.VMEM((tm, d), BF16), pltpu.VMEM((tm, d), F32)],
        compiler_params=_cparams("parallel", "arbitrary"),
        name="ffn_ln",
    )(x, w_in, w_in, w_out, g, b)


def _mm_kernel(x_ref, w_ref, o_ref):
    o_ref[...] = _dot(x_ref[...], w_ref[...]).astype(o_ref.dtype)


def _matmul(x, w, out_dtype, *, bm, bn, name):
    t, k = x.shape
    n = w.shape[1]
    return pl.pallas_call(
        _mm_kernel,
        grid=(t // bm, n // bn),
        in_specs=[pl.BlockSpec((bm, k), lambda i, j: (i, 0)), pl.BlockSpec((k, bn), lambda i, j: (0, j))],
        out_specs=pl.BlockSpec((bm, bn), lambda i, j: (i, j)),
        out_shape=jax.ShapeDtypeStruct((t, n), out_dtype),
        compiler_params=_cparams("parallel", "arbitrary"),
        name=name,
    )(x, w)


def _split3(x):
    h = x.astype(BF16)
    r1 = x - h.astype(F32)
    m = r1.astype(BF16)
    lo = (r1 - m.astype(F32)).astype(BF16)
    return h, m, lo


def _dot3(x, w_bf16):
    h, m, lo = _split3(x)
    return _dot(h, w_bf16) + _dot(m, w_bf16) + _dot(lo, w_bf16)


def _rwkv_kernel(r_ref, k_ref, v_ref, lo_ref, mur_ref, muk_ref, muv_ref, mulo_ref,
                 w0_ref, a0_ref, kk_ref, ka_ref, rk_ref, gg_ref, gb_ref,
                 w2_ref, a2_ref, g2_ref, o_ref,
                 s_ref, pr_ref, pk_ref, pv_ref, plo_ref):
    tb = pl.program_id(2)
    ts = r_ref.shape[0]
    C = RWKV_CHUNK
    H2 = 2 * C
    nch = ts // C

    @pl.when(tb == 0)
    def _():
        s_ref[...] = jnp.zeros_like(s_ref)
        pr_ref[...] = jnp.zeros_like(pr_ref)
        pk_ref[...] = jnp.zeros_like(pk_ref)
        pv_ref[...] = jnp.zeros_like(pv_ref)
        plo_ref[...] = jnp.zeros_like(plo_ref)

    def shift(x_ref, prev_ref, mu_ref):
        x = x_ref[...]
        first = lax.broadcasted_iota(I32, x.shape, 0) == 0
        xp = jnp.where(first, prev_ref[...], pltpu.roll(x, 1, 0))
        prev_ref[...] = x[ts - 1:ts, :]
        return x + (xp - x) * mu_ref[...]

    r = shift(r_ref, pr_ref, mur_ref)
    k = shift(k_ref, pk_ref, muk_ref)
    v = shift(v_ref, pv_ref, muv_ref)
    lo = shift(lo_ref, plo_ref, mulo_ref)

    lane = lax.broadcasted_iota(I32, (LANES, LANES), 1)
    row = lax.broadcasted_iota(I32, (LANES, LANES), 0)
    same_head = (lane < 64) == (row < 64)
    seg_ones = jnp.where(same_head, 1.0, 0.0).astype(BF16)

    wa = lo[:, :LANES]
    zw = w0_ref[...] + _dot(jnp.tanh(wa).astype(BF16), w2_ref[...])
    nz = -zw
    softplus = jnp.maximum(nz, 0.0) + jnp.log1p(jnp.exp(-jnp.abs(nz)))
    logw = -jnp.exp(-softplus - 0.5)
    a = jax.nn.sigmoid(a0_ref[...] + _dot(wa.astype(BF16), a2_ref[...]))
    g = _dot(jax.nn.sigmoid(lo[:, LANES:]).astype(BF16), g2_ref[...])

    kk = k * kk_ref[...]
    n2 = _dot3(kk * kk, seg_ones)
    kkn = kk / jnp.maximum(jnp.sqrt(n2), 1e-12)
    k2 = k * (1.0 + (a - 1.0) * ka_ref[...])

    rin = lax.broadcasted_iota(I32, (ts, LANES), 0) & (C - 1)
    cum = logw
    step = 1
    while step < C:
        cum = cum + jnp.where(rin >= step, pltpu.roll(cum, step, 0), 0.0)
        step *= 2

    aa = -kkn
    bb = kkn * a
    p_inv = jnp.exp(-cum)
    at = aa * jnp.exp(cum - logw)
    rt = r * jnp.exp(cum)
    kt = k2 * p_inv
    bt = bb * p_inv

    head_a = lax.broadcasted_iota(I32, (ts, LANES), 1) < 64
    head_a_c = lax.broadcasted_iota(I32, (C, LANES), 1) < 64

    def by_head(x):
        xb = x.astype(BF16)
        z = jnp.zeros_like(xb)
        return jnp.where(head_a, xb, z), jnp.where(head_a, z, xb)

    def rows(x, c):
        return x[c * C:(c + 1) * C, :]

    def stack(pair, c):
        return jnp.concatenate([rows(pair[0], c), rows(pair[1], c)], axis=0)

    def stack_val(x):
        xb = x.astype(BF16)
        z = jnp.zeros_like(xb)
        return jnp.concatenate([jnp.where(head_a_c, xb, z), jnp.where(head_a_c, z, xb)], axis=0)

    def bdiag(x):
        xb = x.astype(BF16)
        return jnp.where(same_head, jnp.concatenate([xb, xb], axis=0), jnp.zeros((H2, H2), BF16))

    crow = lax.broadcasted_iota(I32, (C, H2), 0)
    ccol = lax.broadcasted_iota(I32, (C, H2), 1)
    ccol = jnp.where(ccol >= C, ccol - C, ccol)
    strict = ccol < crow
    incl = ccol <= crow
    eye_rp = jnp.where(ccol == crow, 1.0, 0.0)
    eye128 = lane == row

    bt_h = by_head(bt)
    kt_h = by_head(kt)
    at_h = by_head(at)
    v_h = by_head(v)
    at_b = at.astype(BF16)
    rt_b = rt.astype(BF16)
    ch = range(nch)

    ar = [jnp.concatenate([rows(at_b, c), rows(rt_b, c)], axis=0) for c in ch]
    xb = [_dot_nt(ar[c], stack(bt_h, c)) for c in ch]
    xk = [_dot_nt(ar[c], stack(kt_h, c)) for c in ch]
    a_ab = [jnp.where(strict, xb[c][:C], 0.0) for c in ch]
    a_rb = [jnp.where(incl, xb[c][C:], 0.0).astype(BF16) for c in ch]
    a_ak = [jnp.where(strict, xk[c][:C], 0.0).astype(BF16) for c in ch]
    a_rk = [jnp.where(incl, xk[c][C:], 0.0).astype(BF16) for c in ch]

    tinv = [eye_rp + a_ab[c] for c in ch]
    apow = a_ab
    n = 1
    while n < C // 2:
        apow = [_dot(apow[c].astype(BF16), bdiag(apow[c])) for c in ch]
        tinv = [tinv[c] + _dot(tinv[c].astype(BF16), bdiag(apow[c])) for c in ch]
        n *= 2
    tinv = [tinv[c].astype(BF16) for c in ch]

    gv = [_dot(a_ak[c], stack(v_h, c)) for c in ch]
    w1 = [_dot(tinv[c], stack(at_h, c)) for c in ch]
    w2 = [_dot(tinv[c], stack_val(gv[c])) for c in ch]
    rq = [(rows(rt, c) + _dot(a_rb[c], stack_val(w1[c]))).astype(BF16) for c in ch]
    y0 = [_dot(a_rk[c], stack(v_h, c)) + _dot(a_rb[c], stack_val(w2[c])) for c in ch]

    cl = [cum[c * C + C - 1:(c + 1) * C, :] for c in ch]
    p_end = [jnp.exp(cl[c] - rows(cum, c)) for c in ch]
    kbt = [jnp.concatenate([rows(k2, c) * p_end[c], rows(bb, c) * p_end[c]], axis=0).T.astype(BF16) for c in ch]
    v_b = v.astype(BF16)
    n_add = [jnp.where(same_head, _dot(kbt[c], jnp.concatenate([rows(v_b, c), w2[c].astype(BF16)], axis=0)), 0.0)
             for c in ch]
    zeros_c = jnp.zeros((C, LANES), BF16)
    m_mat = [(jnp.where(same_head, _dot(kbt[c], jnp.concatenate([zeros_c, w1[c].astype(BF16)], axis=0)), 0.0)
              + jnp.where(eye128, jnp.exp(cl[c]), 0.0)).astype(BF16) for c in ch]

    s_cur = s_ref[...]
    ys = []
    for c in ch:
        s_b = s_cur.astype(BF16)
        ys.append(_dot(rq[c], s_b) + y0[c])
        s_cur = _dot(m_mat[c], s_b) + n_add[c]
    s_ref[...] = s_cur
    y = jnp.concatenate(ys, axis=0)

    inv_n = 1.0 / RWKV_HEAD_DIM
    mu = _dot3(y, seg_ones) * inv_n
    d = y - mu
    var = _dot3(d * d, seg_ones) * inv_n
    yn = d * lax.rsqrt(var + RWKV_GN_EPS) * gg_ref[...] + gb_ref[...]
    bonus = _dot3(r * k2 * rk_ref[...], seg_ones) * v
    o_ref[...] = ((yn + bonus) * g).astype(o_ref.dtype)


def _rwkv(p, mu_pad, w0, a0, k_k, k_a, r_k, gn_g, gn_b, w2p, a2p, g2p, *, batch, seq, ts=1024):
    t = p.shape[0]
    npair = RWKV_HEADS // 2
    nt = seq // ts
    nlora = 3 * RWKV_DIM // RWKV_LORA_PAD

    def col(off):
        return pl.BlockSpec((ts, LANES), lambda b, h, s, off=off: (b * nt + s, off + h))

    def prow(off=0):
        return pl.BlockSpec((1, LANES), lambda b, h, s, off=off: (0, off + h))

    f32 = lambda *shape: pltpu.VMEM(shape, F32)
    return pl.pallas_call(
        _rwkv_kernel,
        grid=(batch, npair, nt),
        in_specs=[
            col(0), col(npair), col(2 * npair),
            pl.BlockSpec((ts, RWKV_LORA_PAD), lambda b, h, s: (b * nt + s, nlora)),
            prow(0), prow(npair), prow(2 * npair),
            pl.BlockSpec((1, RWKV_LORA_PAD), lambda b, h, s: (0, nlora)),
            prow(), prow(), prow(), prow(), prow(), prow(), prow(),
            pl.BlockSpec((LANES, LANES), lambda b, h, s: (0, h)),
            pl.BlockSpec((LANES, LANES), lambda b, h, s: (0, h)),
            pl.BlockSpec((2 * LANES, LANES), lambda b, h, s: (0, h)),
        ],
        out_specs=pl.BlockSpec((ts, LANES), lambda b, h, s: (b * nt + s, h)),
        out_shape=jax.ShapeDtypeStruct((t, RWKV_DIM), BF16),
        scratch_shapes=[
            f32(LANES, LANES), f32(1, LANES), f32(1, LANES), f32(1, LANES), f32(1, RWKV_LORA_PAD),
        ],
        compiler_params=_cparams("parallel", "parallel", "arbitrary"),
        name="rwkv7",
    )(p, p, p, p, mu_pad, mu_pad, mu_pad, mu_pad, w0, a0, k_k, k_a, r_k, gn_g, gn_b, w2p, a2p, g2p)


def _sgu_kernel(p_ref, lg_ref, lb_ref, w_ref, bs_ref, o_ref):
    tm = p_ref.shape[0]
    p = p_ref[...]
    z = 0.5 * p * (1.0 + lax.erf(p * math.sqrt(0.5)))
    u = z[:, :SG_DIM]
    v = _layer_norm(z[:, SG_DIM:], lg_ref[...], lb_ref[...], LN_EPS).astype(BF16)
    rowi = lax.broadcasted_iota(I32, (SG_CHUNK, SG_CHUNK), 0)
    colj = lax.broadcasted_iota(I32, (SG_CHUNK, SG_CHUNK), 1)
    causal = colj <= rowi
    for g in range(SG_GROUPS):
        wg = jnp.where(causal, w_ref[g], 0.0).astype(BF16)
        gs = slice(g * SG_GROUP_DIM, (g + 1) * SG_GROUP_DIM)
        for c in range(tm // SG_CHUNK):
            rs = slice(c * SG_CHUNK, (c + 1) * SG_CHUNK)
            mixed = _dot(wg, v[rs, gs]) + bs_ref[:, gs]
            o_ref[rs, gs] = (u[rs, gs] * mixed).astype(o_ref.dtype)


def _sgu(p, ln_g, ln_b, w_s, bias_full, *, tm=512):
    t = p.shape[0]
    return pl.pallas_call(
        _sgu_kernel,
        grid=(t // tm,),
        in_specs=[
            pl.BlockSpec((tm, SG_COLS), lambda i: (i, 0)),
            pl.BlockSpec((1, SG_DIM), lambda i: (0, 0)),
            pl.BlockSpec((1, SG_DIM), lambda i: (0, 0)),
            pl.BlockSpec((SG_GROUPS, SG_CHUNK, SG_CHUNK), lambda i: (0, 0, 0)),
            pl.BlockSpec((SG_CHUNK, SG_DIM), lambda i: (0, 0)),
        ],
        out_specs=pl.BlockSpec((tm, SG_DIM), lambda i: (i, 0)),
        out_shape=jax.ShapeDtypeStruct((t, SG_DIM), BF16),
        compiler_params=_cparams("parallel"),
        name="sgu",
    )(p, ln_g, ln_b, w_s, bias_full)


def _kidx_kernel(x_ref, g_ref, b_ref, o_ref):
    x = x_ref[...]
    lane = lax.broadcasted_iota(I32, x.shape, 1)
    x2 = jnp.where(lane < IDX_HEAD_DIM, x, pltpu.roll(x, IDX_HEAD_DIM, 1))
    o_ref[...] = _layer_norm(x2, g_ref[...], b_ref[...], LN_EPS).astype(o_ref.dtype)


def _kidx_ln(x, g2, b2, *, tm=2048):
    t = x.shape[0]
    return pl.pallas_call(
        _kidx_kernel,
        grid=(t // tm,),
        in_specs=[pl.BlockSpec((tm, LANES), lambda i: (i, 0)),
                  pl.BlockSpec((1, LANES), lambda i: (0, 0)),
                  pl.BlockSpec((1, LANES), lambda i: (0, 0))],
        out_specs=pl.BlockSpec((tm, LANES), lambda i: (i, 0)),
        out_shape=jax.ShapeDtypeStruct((t, LANES), BF16),
        compiler_params=_cparams("parallel"),
        name="kidx_ln",
    )(x, g2, b2)


def _dsa_kernel(qi_ref, q_ref, k_ref, vt_ref, kx_ref, wt_ref, b3_ref, o_ref,
                rt_ref, qt_ref, key_ref, m_ref, l_ref, acc_ref, thr_ref, cnt_ref, *, topk):
    i = pl.program_id(1)
    QB = Q_BLOCK
    head_a = lax.broadcasted_iota(I32, (QB, LANES), 1) < 64

    def pair_rhs(x):
        zero = jnp.zeros_like(x)
        return jnp.concatenate([jnp.where(head_a, x, zero), jnp.where(head_a, zero, x)], axis=0)

    n_ip = IDX_HEADS // 2
    n_ap = ATT_HEADS // 2
    for hp in range(n_ip):
        rt_ref[hp] = pair_rhs(qi_ref[:, hp * LANES:(hp + 1) * LANES])
    for hp in range(n_ap):
        qt_ref[hp] = pair_rhs(q_ref[:, hp * LANES:(hp + 1) * LANES])

    scale = (IDX_HEAD_DIM ** -0.5) * (IDX_HEADS ** -0.5)
    wt = wt_ref[...] * scale
    wrows = [jnp.concatenate([wt[2 * hp:2 * hp + 1, :], wt[2 * hp + 1:2 * hp + 2, :]], axis=1)
             for hp in range(n_ip)]

    int_min = jnp.int32(-2 ** 31)

    SK = DSA_SUPER * QB
    n_super = (i + DSA_SUPER) // DSA_SUPER
    krow = lax.broadcasted_iota(I32, (SK, LANES), 0)
    qlane = lax.broadcasted_iota(I32, (SK, LANES), 1)

    def score_step(sc_i, carry):
        ks = pl.ds(pl.multiple_of(sc_i * SK, SK), SK)
        kc = kx_ref[ks, :]
        acc = jnp.maximum(_dot_nt(kc, rt_ref[0]), 0.0) * wrows[0]
        for hp in range(1, n_ip):
            acc = acc + jnp.maximum(_dot_nt(kc, rt_ref[hp]), 0.0) * wrows[hp]
        sc = acc[:, :QB] + acc[:, QB:]
        bits = lax.bitcast_convert_type(sc, I32)
        key = bits ^ ((bits >> 31) & jnp.int32(0x7FFFFFFF))
        valid = (sc_i * SK + krow) <= (i * QB + qlane)
        key_ref[ks, :] = jnp.where(valid, key, int_min)
        return carry

    lax.fori_loop(0, n_super, score_step, 0)

    def count_ge(cand):
        def count_step(sc_i, cnt):
            ks = pl.ds(pl.multiple_of(sc_i * SK, SK), SK)
            ge = jnp.where(key_ref[ks, :] >= cand, 1, 0)
            parts = [ge[r8 * 8:(r8 + 1) * 8, :] for r8 in range(SK // 8)]
            while len(parts) > 1:
                parts = [parts[j] + parts[j + 1] for j in range(0, len(parts), 2)]
            return cnt + parts[0]

        cnt8 = lax.fori_loop(0, n_super, count_step, jnp.zeros((8, LANES), I32))
        return jnp.sum(cnt8, axis=0, keepdims=True)

    def bit_step(bi, carry):
        t, ct = carry
        cand = t + lax.shift_left(jnp.int32(1), 31 - bi)
        cnt = count_ge(cand)
        take = cnt >= topk
        return jnp.where(take, cand, t), jnp.where(take, cnt.astype(F32), ct)

    thr_ref[...] = jnp.full((1, LANES), int_min, I32)
    cnt_ref[...] = (i * QB + lax.broadcasted_iota(I32, (1, LANES), 1) + 1).astype(F32)
    for grp in range(32 // DSA_BIT_GROUP):
        @pl.when(jnp.max(cnt_ref[...]) > topk)
        def _():
            t, ct = lax.fori_loop(grp * DSA_BIT_GROUP, (grp + 1) * DSA_BIT_GROUP, bit_step,
                                  (thr_ref[...], cnt_ref[...]))
            thr_ref[...] = t
            cnt_ref[...] = ct
    thr = thr_ref[...]

    thr = jnp.maximum(thr, int_min + 1)
    m_ref[...] = jnp.full(m_ref.shape, NEG_BIG, F32)
    l_ref[...] = jnp.zeros(l_ref.shape, F32)
    acc_ref[...] = jnp.zeros(acc_ref.shape, F32)
    att_scale = ATT_HEAD_DIM ** -0.5

    def attend_step(sc_i, carry):
        base = sc_i * DSA_SUPER
        ks = pl.ds(pl.multiple_of(sc_i * SK, SK), SK)
        madd = jnp.where(key_ref[ks, :] >= thr, 0.0, 2.0 * NEG_BIG)
        madd2 = jnp.concatenate([madd, madd], axis=1)
        raw = [_dot_nt(k_ref[ks, hp * LANES:(hp + 1) * LANES], qt_ref[hp]) for hp in range(n_ap)]
        kinds = [jnp.clip(i - (base + j), 0, 2) for j in range(DSA_SUPER)]
        ps = []
        for hp in range(n_ap):
            bias = jnp.concatenate([b3_ref[kinds[j], hp] for j in range(DSA_SUPER)], axis=0)
            s = raw[hp] * att_scale + bias + madd2
            m_old = m_ref[hp]
            m_new = jnp.maximum(m_old, jnp.max(s, axis=0, keepdims=True))
            p = jnp.exp(s - m_new)
            alpha = jnp.exp(m_old - m_new)
            l_ref[hp] = alpha * l_ref[hp] + jnp.sum(p, axis=0, keepdims=True)
            m_ref[hp] = m_new
            ps.append((p.astype(BF16), alpha))
        for hp in range(n_ap):
            vt = jnp.concatenate([vt_ref[0, base + j, hp * LANES:(hp + 1) * LANES, :] for j in range(DSA_SUPER)],
                                 axis=1)
            acc_ref[hp] = acc_ref[hp] * ps[hp][1] + _dot(vt, ps[hp][0])
        return carry

    lax.fori_loop(0, n_super, attend_step, 0)

    for hp in range(n_ap):
        o2 = acc_ref[hp] * (1.0 / l_ref[hp])
        pair_t = jnp.concatenate([o2[:64, :QB], o2[64:, QB:]], axis=0)
        o_ref[:, hp * LANES:(hp + 1) * LANES] = pair_t.T.astype(o_ref.dtype)


def _dsa(p_main, vt, kidx2, w_t, bias3, *, batch, seq):
    t = p_main.shape[0]
    nb = seq // Q_BLOCK
    assert nb % DSA_SUPER == 0
    top_k = min(TOPK_MAX, seq // 4)
    n_ip, n_ap = IDX_HEADS // 2, ATT_HEADS // 2
    qi_w = IDX_HEADS * IDX_HEAD_DIM
    return pl.pallas_call(
        functools.partial(_dsa_kernel, topk=top_k),
        grid=(batch, nb),
        in_specs=[
            pl.BlockSpec((Q_BLOCK, qi_w), lambda b, i: (b * nb + i, 0)),
            pl.BlockSpec((Q_BLOCK, ATT_DIM), lambda b, i: (b * nb + i, qi_w // ATT_DIM)),
            pl.BlockSpec((seq, ATT_DIM), lambda b, i: (b, qi_w // ATT_DIM + 1)),
            pl.BlockSpec((1, nb, ATT_DIM, Q_BLOCK), lambda b, i: (b, 0, 0, 0)),
            pl.BlockSpec((seq, LANES), lambda b, i: (b, 0)),
            pl.BlockSpec((IDX_HEADS, Q_BLOCK), lambda b, i: (0, b * nb + i)),
            pl.BlockSpec((3, n_ap, Q_BLOCK, 2 * Q_BLOCK), lambda b, i: (0, 0, 0, 0)),
        ],
        out_specs=pl.BlockSpec((Q_BLOCK, ATT_DIM), lambda b, i: (b * nb + i, 0)),
        out_shape=jax.ShapeDtypeStruct((t, ATT_DIM), BF16),
        scratch_shapes=[
            pltpu.VMEM((n_ip, 2 * Q_BLOCK, LANES), BF16),
            pltpu.VMEM((n_ap, 2 * Q_BLOCK, LANES), BF16),
            pltpu.VMEM((seq, Q_BLOCK), I32),
            pltpu.VMEM((n_ap, 1, 2 * Q_BLOCK), F32),
            pltpu.VMEM((n_ap, 1, 2 * Q_BLOCK), F32),
            pltpu.VMEM((n_ap, LANES, 2 * Q_BLOCK), F32),
            pltpu.VMEM((1, LANES), I32),
            pltpu.VMEM((1, LANES), F32),
        ],
        compiler_params=_cparams("parallel", "arbitrary"),
        name="dsa",
    )(p_main, p_main, p_main, vt, kidx2, w_t, bias3)


def _merge_kernel(yr_ref, ys_ref, ya_ref, pg_ref, bg_ref, wb_ref, wo_ref, x_ref, g_ref, b_ref, o_ref):
    d = D_MODEL
    z = _dot(yr_ref[...], wb_ref[0:RWKV_DIM, :])
    merged = jax.nn.sigmoid(pg_ref[:, 0:d] + bg_ref[:, 0:d]) * z
    z = _dot(ys_ref[...], wb_ref[RWKV_DIM:RWKV_DIM + SG_DIM, :])
    merged = merged + jax.nn.sigmoid(pg_ref[:, d:2 * d] + bg_ref[:, d:2 * d]) * z
    z = _dot(ya_ref[...], wb_ref[RWKV_DIM + SG_DIM:, :])
    merged = merged + jax.nn.sigmoid(pg_ref[:, 2 * d:] + bg_ref[:, 2 * d:]) * z
    mix = _dot(merged.astype(BF16), wo_ref[...])
    y = DEEPNORM_ALPHA * x_ref[...] + mix
    o_ref[...] = _layer_norm(y, g_ref[...], b_ref[...], LN_EPS)


def _merge(y_rwkv, y_sg, y_att, p_gate, b_gate, w_branch, w_o, x, g, b, *, tm=256):
    t, d = x.shape
    rowblk = lambda w: pl.BlockSpec((tm, w), lambda i: (i, 0))
    const = lambda r, c: pl.BlockSpec((r, c), lambda i: (0, 0))
    return pl.pallas_call(
        _merge_kernel,
        grid=(t // tm,),
        in_specs=[rowblk(RWKV_DIM), rowblk(SG_DIM), rowblk(ATT_DIM), rowblk(GATE_COLS), const(1, GATE_COLS),
                  const(d, d), const(d, d), rowblk(d), const(1, d), const(1, d)],
        out_specs=rowblk(d),
        out_shape=jax.ShapeDtypeStruct((t, d), F32),
        compiler_params=_cparams("parallel"),
        name="merge",
    )(y_rwkv, y_sg, y_att, p_gate, b_gate, w_branch, w_o, x, g, b)


def _rel_bucket(dist):
    max_exact = REL_BUCKETS // 2
    d_f = jnp.maximum(dist, 1).astype(F32)
    large = max_exact + (jnp.log(d_f / max_exact) / math.log(REL_MAX_DIST / max_exact)
                         * (REL_BUCKETS - max_exact)).astype(I32)
    large = jnp.minimum(large, REL_BUCKETS - 1)
    return jnp.where(dist < max_exact, dist, large)


def _bias_tables(rel_bias):
    n_ap = ATT_HEADS // 2
    rk = jnp.arange(Q_BLOCK, dtype=I32)[:, None]
    lq = jnp.arange(2 * Q_BLOCK, dtype=I32)[None, :] % Q_BLOCK
    per_pair = [jnp.concatenate([jnp.broadcast_to(rel_bias[:, 2 * hp, None], (REL_BUCKETS, Q_BLOCK)),
                                 jnp.broadcast_to(rel_bias[:, 2 * hp + 1, None], (REL_BUCKETS, Q_BLOCK))], axis=1)
                for hp in range(n_ap)]
    tiles = []
    for delta in (0, 1, 2):
        bucket = _rel_bucket(jnp.maximum(delta * Q_BLOCK + lq - rk, 0))
        pairs = []
        for hp in range(n_ap):
            t = jnp.zeros((Q_BLOCK, 2 * Q_BLOCK), F32)
            for bkt in range(REL_BUCKETS):
                t = t + jnp.where(bucket == bkt, per_pair[hp][bkt][None, :], 0.0)
            pairs.append(t)
        tiles.append(jnp.stack(pairs))
    return jnp.stack(tiles)


def _pad_rows(w, rows):
    return jnp.concatenate([w, jnp.zeros((rows - w.shape[0], w.shape[1]), w.dtype)], axis=0)


def kernel(x, ffn1_w_in, ffn1_w_out, ln1_g, ln1_b, w_in, b_gate, rwkv_mu, rwkv_w0, rwkv_w2, rwkv_a0, rwkv_a2, rwkv_g2, rwkv_k_k, rwkv_k_a, rwkv_r_k, rwkv_gn_g, rwkv_gn_b, sg_ln_g, sg_ln_b, sg_w, sg_b, idx_ln_g, idx_ln_b, rel_bias, w_branch, w_o, ln2_g, ln2_b, ffn2_w_in, ffn2_w_out, ln3_g, ln3_b):
    batch, seq, d = x.shape
    t = batch * seq
    nb = seq // Q_BLOCK
    depth = w_in.shape[0]
    xf = x.reshape(t, d)
    bias3 = _bias_tables(rel_bias)
    row2 = lambda v: v.reshape(1, -1)

    for l in range(depth):
        xf, xb = _ffn_ln(xf, ffn1_w_in[l].astype(BF16), ffn1_w_out[l].astype(BF16), row2(ln1_g[l]), row2(ln1_b[l]))

        w = w_in[l]
        c0 = RWKV_COLS
        c1 = c0 + SG_COLS
        c2 = c1 + ATT_COLS
        w_rwkv = jnp.concatenate([w[:, :c0], jnp.zeros((d, RWKV_COLS_PAD - RWKV_COLS), w.dtype)], axis=1).astype(BF16)
        w_sg = w[:, c0:c1].astype(BF16)
        wa = w[:, c1:c2]
        qkv_w = 3 * ATT_DIM
        w_attm = jnp.concatenate([wa[:, qkv_w:qkv_w + IDX_HEADS * IDX_HEAD_DIM], wa[:, :qkv_w]], axis=1).astype(BF16)
        n_misc = IDX_HEAD_DIM + IDX_HEADS
        w_attx = jnp.concatenate([wa[:, ATT_COLS - n_misc:], jnp.zeros((d, LANES - n_misc), w.dtype)], axis=1).astype(BF16)
        w_gate = w[:, c2:].astype(BF16)

        p_rwkv = _matmul(xb, w_rwkv, F32, bm=1024, bn=1152, name="proj_rwkv")
        p_sg = _matmul(xb, w_sg, F32, bm=1024, bn=1024, name="proj_sg")
        p_attm = _matmul(xb, w_attm, BF16, bm=1024, bn=1280, name="proj_att")
        p_attx = _matmul(xb, w_attx, F32, bm=1024, bn=LANES, name="proj_attx")
        p_gate = _matmul(xb, w_gate, F32, bm=1024, bn=1024, name="proj_gate")

        mu_pad = jnp.concatenate([rwkv_mu[l], jnp.zeros((RWKV_COLS_PAD - RWKV_COLS,), F32)]).reshape(1, -1)
        w2p = _pad_rows(rwkv_w2[l], LANES).astype(BF16)
        a2p = jnp.concatenate([jnp.zeros((DECAY_LORA, RWKV_DIM), F32), rwkv_a2[l]], axis=0).astype(BF16)
        g2p = _pad_rows(rwkv_g2[l], 2 * LANES).astype(BF16)
        y_rwkv = _rwkv(p_rwkv, mu_pad, row2(rwkv_w0[l]), row2(rwkv_a0[l]), row2(rwkv_k_k[l]), row2(rwkv_k_a[l]),
                       row2(rwkv_r_k[l]), row2(rwkv_gn_g[l]), row2(rwkv_gn_b[l]), w2p, a2p, g2p,
                       batch=batch, seq=seq)

        sg_bias = jnp.repeat(sg_b[l].T, SG_GROUP_DIM, axis=1)
        y_sg = _sgu(p_sg, row2(sg_ln_g[l]), row2(sg_ln_b[l]), sg_w[l], sg_bias)

        kidx2 = _kidx_ln(p_attx, row2(jnp.tile(idx_ln_g[l], 2)), row2(jnp.tile(idx_ln_b[l], 2)))
        w_t = p_attx[:, IDX_HEAD_DIM:IDX_HEAD_DIM + IDX_HEADS].T
        v_t = p_attm[:, ATT_MAIN_COLS - ATT_DIM:].reshape(batch, nb, Q_BLOCK, ATT_DIM).transpose(0, 1, 3, 2)
        y_att = _dsa(p_attm, v_t, kidx2, w_t, bias3, batch=batch, seq=seq)

        xf = _merge(y_rwkv, y_sg, y_att, p_gate, row2(b_gate[l]), w_branch[l].astype(BF16), w_o[l].astype(BF16),
                    xf, row2(ln2_g[l]), row2(ln2_b[l]))

        xf, _ = _ffn_ln(xf, ffn2_w_in[l].astype(BF16), ffn2_w_out[l].astype(BF16), row2(ln3_g[l]), row2(ln3_b[l]))

    return xf.reshape(batch, seq, d)
```
